```python
import math
import jax, jax.numpy as jnp
from jax import lax
import numpy as np


D_MODEL = 1024
BATCH = 2
SEQ = 16384
DEPTH = 2

CHUNK = 64
EPS = 1e-6
NEG = -1e30
N_EVEN = (DEPTH + 1) // 2
N_ODD = DEPTH // 2

SSD_HEADS = 16
SSD_HEAD_DIM = 64
SSD_INNER = SSD_HEADS * SSD_HEAD_DIM
SSD_GROUPS = 2
SSD_HPG = SSD_HEADS // SSD_GROUPS
SSD_STATE = 128
SSD_GN = SSD_GROUPS * SSD_STATE
SSD_XBC = SSD_INNER + 2 * SSD_GN
SSD_CONV = 4

ML_HEADS = 4
ML_QK_DIM = 128
ML_V_DIM = 256
ML_QK = ML_HEADS * ML_QK_DIM
ML_INNER = ML_HEADS * ML_V_DIM
ML_CONV = 4

DA_HEADS = 8
DA_QK_DIM = 64
DA_V_DIM = 2 * DA_QK_DIM
DA_QK = DA_HEADS * 2 * DA_QK_DIM
DA_INNER = DA_HEADS * DA_V_DIM
Q_BLOCK = 128
ALIBI_SLOPES = tuple(2.0 ** (-8.0 * (h + 1) / DA_HEADS) for h in range(DA_HEADS))

AB_SIZES = (SSD_INNER, SSD_XBC, SSD_HEADS, ML_INNER, ML_QK, ML_QK, ML_INNER, ML_HEADS, ML_HEADS, ML_INNER)
AB_SPLITS = tuple(int(s) for s in np.cumsum(AB_SIZES)[:-1])
AB_COLS = int(sum(AB_SIZES))
AB_OUT = SSD_INNER + ML_INNER
C_SIZES = (DA_QK, DA_QK, DA_INNER, DA_INNER)
C_SPLITS = tuple(int(s) for s in np.cumsum(C_SIZES)[:-1])
C_COLS = int(sum(C_SIZES))

kernel_name = 'hybrid_ssd_mlstm_diffattn_chunk_causal'


def _f32(a):
    return a.astype(jnp.float32)


def _rms(x):
    return x * lax.rsqrt(jnp.mean(x * x, axis=-1, keepdims=True) + EPS)


def _rmsnorm(x, g):
    xf = _f32(x)
    return (_rms(xf) * _f32(g)).astype(x.dtype)


def _causal_conv(x, w, b):
    k = w.shape[0]
    c = x.shape[-1]
    y = lax.conv_general_dilated(x, w[:, None, :], window_strides=(1,), padding=[(k - 1, 0)],
                                 dimension_numbers=('NWC', 'WIO', 'NWC'), feature_group_count=c)
    return y + b


def _ssd(xbc, dt_raw, z, conv_w, conv_b, dt_bias, a_log, d_skip, norm_g):
    bsz, seqlen, _ = xbc.shape
    nc = seqlen // CHUNK
    xbc = jax.nn.silu(_causal_conv(xbc, conv_w, conv_b))
    xs, bm, cm = jnp.split(xbc, [SSD_INNER, SSD_INNER + SSD_GN], axis=-1)
    xs = xs.reshape(bsz, nc, CHUNK, SSD_GROUPS, SSD_HPG, SSD_HEAD_DIM)
    bm = bm.reshape(bsz, nc, CHUNK, SSD_GROUPS, SSD_STATE)
    cm = cm.reshape(bsz, nc, CHUNK, SSD_GROUPS, SSD_STATE)
    dt = jax.nn.softplus(dt_raw + dt_bias).reshape(bsz, nc, CHUNK, SSD_GROUPS, SSD_HPG)
    a = -jnp.exp(a_log).reshape(SSD_GROUPS, SSD_HPG)
    a_cs = jnp.cumsum(dt * a, axis=2)
    xdt = xs * dt[..., None]
    causal = jnp.tril(jnp.ones((CHUNK, CHUNK), dtype=bool))[None, None, :, :, None, None]
    seg = a_cs[:, :, :, None] - a_cs[:, :, None, :]
    decay = jnp.where(causal, jnp.exp(jnp.where(causal, seg, 0.0)), 0.0)
    cb = jnp.einsum('bclgn,bcsgn->bclsg', cm, bm)
    y_diag = jnp.einsum('bclsg,bclsgr,bcsgrp->bclgrp', cb, decay, xdt)

    def step(state, inp):
        xdt_c, b_c, c_c, acs_c = inp
        y_off = jnp.einsum('blgn,bgrpn->blgrp', c_c, state) * jnp.exp(acs_c)[..., None]
        to_end = jnp.exp(acs_c[:, -1:] - acs_c)
        new_state = state * jnp.exp(acs_c[:, -1])[..., None, None] + \
            jnp.einsum('blgn,blgr,blgrp->bgrpn', b_c, to_end, xdt_c)
        return new_state, y_off

    state0 = jnp.zeros((bsz, SSD_GROUPS, SSD_HPG, SSD_HEAD_DIM, SSD_STATE), jnp.float32)
    mv = lambda t: jnp.moveaxis(t, 1, 0)
    _, y_off = lax.scan(step, state0, (mv(xdt), mv(bm), mv(cm), mv(a_cs)))
    y = y_diag + jnp.moveaxis(y_off, 0, 1) + xs * d_skip.reshape(SSD_GROUPS, SSD_HPG)[:, :, None]
    y = y.reshape(bsz, seqlen, SSD_INNER) * jax.nn.silu(z)
    yg = _rms(y.reshape(bsz, seqlen, SSD_GROUPS, SSD_INNER // SSD_GROUPS))
    return yg.reshape(bsz, seqlen, SSD_INNER) * norm_g


def _mlstm(q, k, v, i_raw, f_raw, o_raw, z, conv_w, conv_b, i_bias, f_bias, norm_g):
    bsz, seqlen, _ = v.shape
    nc = seqlen // CHUNK
    qk = jax.nn.silu(_causal_conv(jnp.concatenate([q, k], axis=-1), conv_w, conv_b))
    q, k = jnp.split(qk, 2, axis=-1)
    q = q.reshape(bsz, nc, CHUNK, ML_HEADS, ML_QK_DIM)
    k = k.reshape(bsz, nc, CHUNK, ML_HEADS, ML_QK_DIM) * (ML_QK_DIM ** -0.5)
    v = v.reshape(bsz, nc, CHUNK, ML_HEADS, ML_V_DIM)
    ig = (i_raw + i_bias).reshape(bsz, nc, CHUNK, ML_HEADS)
    bcum = jnp.cumsum(jax.nn.log_sigmoid(f_raw + f_bias).reshape(bsz, nc, CHUNK, ML_HEADS), axis=2)
    causal = jnp.tril(jnp.ones((CHUNK, CHUNK), dtype=bool))[None, :, :, None]

    def step(carry, inp):
        c_mem, n_mem, m_prev = carry
        q_c, k_c, v_c, ig_c, b_c = inp
        d_log = jnp.where(causal, b_c[:, :, None] - b_c[:, None, :] + ig_c[:, None, :], NEG)
        inter_log = b_c + m_prev[:, None]
        m_t = jnp.maximum(inter_log, jnp.max(d_log, axis=2))
        w_intra = jnp.exp(d_log - m_t[:, :, None])
        w_inter = jnp.exp(inter_log - m_t)
        s = jnp.einsum('bthd,bshd->btsh', q_c, k_c) * w_intra
        num = jnp.einsum('btsh,bshv->bthv', s, v_c) + \
            w_inter[..., None] * jnp.einsum('bthd,bhdv->bthv', q_c, c_mem)
        den = jnp.sum(s, axis=2) + w_inter * jnp.einsum('bthd,bhd->bth', q_c, n_mem)
        h = num / jnp.maximum(jnp.abs(den), jnp.exp(-m_t))[..., None]
        b_end = b_c[:, -1]
        g_log = b_end[:, None] - b_c + ig_c
        m_new = jnp.maximum(b_end + m_prev, jnp.max(g_log, axis=1))
        g = jnp.exp(g_log - m_new[:, None])
        dec = jnp.exp(b_end + m_prev - m_new)
        c_new = dec[..., None, None] * c_mem + jnp.einsum('bsh,bshd,bshv->bhdv', g, k_c, v_c)
        n_new = dec[..., None] * n_mem + jnp.einsum('bsh,bshd->bhd', g, k_c)
        return (c_new, n_new, m_new), h

    carry0 = (jnp.zeros((bsz, ML_HEADS, ML_QK_DIM, ML_V_DIM), jnp.float32),
              jnp.zeros((bsz, ML_HEADS, ML_QK_DIM), jnp.float32),
              jnp.zeros((bsz, ML_HEADS), jnp.float32))
    mv = lambda t: jnp.moveaxis(t, 1, 0)
    _, h = lax.scan(step, carry0, (mv(q), mv(k), mv(v), mv(ig), mv(bcum)))
    h = jnp.moveaxis(h, 0, 1).reshape(bsz, seqlen, ML_HEADS, ML_V_DIM)
    h = h * jax.nn.sigmoid(o_raw).reshape(bsz, seqlen, ML_HEADS, ML_V_DIM)
    h = _rms(h).reshape(bsz, seqlen, ML_INNER) * norm_g
    return h * jax.nn.silu(z)


def _diff_attn(q, k, v, z, lam_q1, lam_k1, lam_q2, lam_k2, subln_g, lambda_init):
    bsz, seqlen, _ = q.shape
    q = q.reshape(bsz, seqlen, DA_HEADS, 2, DA_QK_DIM)
    k = k.reshape(bsz, seqlen, DA_HEADS, 2, DA_QK_DIM)
    v = v.reshape(bsz, seqlen, DA_HEADS, DA_V_DIM)
    lam = jnp.exp(jnp.sum(lam_q1 * lam_k1)) - jnp.exp(jnp.sum(lam_q2 * lam_k2)) + lambda_init
    slopes = jnp.asarray(ALIBI_SLOPES, jnp.float32)
    key_pos = jnp.arange(seqlen)
    key_chunk = key_pos // CHUNK
    nb = seqlen // Q_BLOCK
    qb = jnp.moveaxis(q.reshape(bsz, nb, Q_BLOCK, DA_HEADS, 2, DA_QK_DIM), 1, 0)
    scale = DA_QK_DIM ** -0.5

    def block(args):
        q_blk, blk = args
        q_pos = blk * Q_BLOCK + jnp.arange(Q_BLOCK)
        s = jnp.einsum('bqhjd,bkhjd->bhjqk', q_blk, k) * scale
        dist = jnp.abs(q_pos[:, None] - key_pos[None, :]).astype(jnp.float32)
        allowed = key_chunk[None, :] <= (q_pos // CHUNK)[:, None]
        s = s - (slopes[:, None, None] * dist)[None, :, None]
        p = jax.nn.softmax(jnp.where(allowed, s, -jnp.inf), axis=-1)
        a = p[:, :, 0] - lam * p[:, :, 1]
        return jnp.einsum('bhqk,bkhv->bqhv', a, v)

    o = lax.map(block, (qb, jnp.arange(nb)))
    o = jnp.moveaxis(o, 0, 1).reshape(bsz, seqlen, DA_HEADS, DA_V_DIM)
    o = _rms(o) * subln_g * (1.0 - lambda_init)
    return o.reshape(bsz, seqlen, DA_INNER) * jax.nn.silu(z)


def setup_inputs(seed: int = 0) -> dict:
    key = jax.random.key(seed)
    ks = jax.random.split(key, 24)
    nrm = lambda k, shape, sc: jax.random.normal(k, shape, jnp.float32) * sc
    u = jax.random.uniform(ks[5], (N_EVEN, SSD_HEADS), jnp.float32)
    dt0 = jnp.exp(u * (math.log(0.1) - math.log(0.001)) + math.log(0.001))
    return {
        'x': nrm(ks[0], (BATCH, SEQ, D_MODEL), 1.0),
        'ab_norm_g': 1.0 + nrm(ks[1], (N_EVEN, D_MODEL), 0.02),
        'ab_w_in': nrm(ks[2], (N_EVEN, D_MODEL, AB_COLS), D_MODEL ** -0.5),
        'ab_ssd_conv_w': nrm(ks[3], (N_EVEN, SSD_CONV, SSD_XBC), SSD_CONV ** -0.5),
        'ab_ssd_conv_b': nrm(ks[4], (N_EVEN, SSD_XBC), 0.01),
        'ab_dt_bias': dt0 + jnp.log(-jnp.expm1(-dt0)),
        'ab_a_log': jnp.log(jax.random.uniform(ks[6], (N_EVEN, SSD_HEADS), jnp.float32, 1.0, 16.0)),
        'ab_d_skip': 1.0 + nrm(ks[7], (N_EVEN, SSD_HEADS), 0.1),
        'ab_ssd_norm_g': 1.0 + nrm(ks[8], (N_EVEN, SSD_INNER), 0.02),
        'ab_ml_conv_w': nrm(ks[9], (N_EVEN, ML_CONV, 2 * ML_QK), ML_CONV ** -0.5),
        'ab_ml_conv_b': nrm(ks[10], (N_EVEN, 2 * ML_QK), 0.01),
        'ab_i_bias': nrm(ks[11], (N_EVEN, ML_HEADS), 0.1),
        'ab_f_bias': jnp.linspace(3.0, 6.0, ML_HEADS, dtype=jnp.float32)[None] + nrm(ks[12], (N_EVEN, ML_HEADS), 0.1),
        'ab_ml_norm_g': 1.0 + nrm(ks[13], (N_EVEN, ML_INNER), 0.02),
        'ab_w_out': nrm(ks[14], (N_EVEN, AB_OUT, D_MODEL), AB_OUT ** -0.5),
        'c_norm_g': 1.0 + nrm(ks[15], (N_ODD, D_MODEL), 0.02),
        'c_w_in': nrm(ks[16], (N_ODD, D_MODEL, C_COLS), D_MODEL ** -0.5),
        'c_lam_q1': nrm(ks[17], (N_ODD, DA_QK_DIM), 0.1),
        'c_lam_k1': nrm(ks[18], (N_ODD, DA_QK_DIM), 0.1),
        'c_lam_q2': nrm(ks[19], (N_ODD, DA_QK_DIM), 0.1),
        'c_lam_k2': nrm(ks[20], (N_ODD, DA_QK_DIM), 0.1),
        'c_subln_g': 1.0 + nrm(ks[21], (N_ODD, DA_V_DIM), 0.02),
        'c_w_out': nrm(ks[22], (N_ODD, DA_INNER, D_MODEL), DA_INNER ** -0.5),
        'final_norm_g': 1.0 + nrm(ks[23], (D_MODEL,), 0.02),
    }


def reference(x, ab_norm_g, ab_w_in, ab_ssd_conv_w, ab_ssd_conv_b, ab_dt_bias, ab_a_log, ab_d_skip,
              ab_ssd_norm_g, ab_ml_conv_w, ab_ml_conv_b, ab_i_bias, ab_f_bias, ab_ml_norm_g, ab_w_out,
              c_norm_g, c_w_in, c_lam_q1, c_lam_k1, c_lam_q2, c_lam_k2, c_subln_g, c_w_out,
              final_norm_g):
    for layer in range(DEPTH):
        if layer % 2 == 0:
            i = layer // 2
            h = _f32(_rmsnorm(x, ab_norm_g[i]))
            proj = h @ _f32(ab_w_in[i])
            z_a, xbc, dt_raw, z_b, q, k, v, ig, fg, og = jnp.split(proj, AB_SPLITS, axis=-1)
            y_a = _ssd(xbc, dt_raw, z_a, _f32(ab_ssd_conv_w[i]), _f32(ab_ssd_conv_b[i]), _f32(ab_dt_bias[i]),
                       _f32(ab_a_log[i]), _f32(ab_d_skip[i]), _f32(ab_ssd_norm_g[i]))
            y_b = _mlstm(q, k, v, ig, fg, og, z_b, _f32(ab_ml_conv_w[i]), _f32(ab_ml_conv_b[i]),
                         _f32(ab_i_bias[i]), _f32(ab_f_bias[i]), _f32(ab_ml_norm_g[i]))
            y = jnp.concatenate([y_a, y_b], axis=-1) @ _f32(ab_w_out[i])
        else:
            i = layer // 2
            h = _f32(_rmsnorm(x, c_norm_g[i]))
            proj = h @ _f32(c_w_in[i])
            q, k, v, z = jnp.split(proj, C_SPLITS, axis=-1)
            lambda_init = 0.8 - 0.6 * math.exp(-0.3 * layer)
            y_c = _diff_attn(q, k, v, z, _f32(c_lam_q1[i]), _f32(c_lam_k1[i]), _f32(c_lam_q2[i]),
                             _f32(c_lam_k2[i]), _f32(c_subln_g[i]), lambda_init)
            y = y_c @ _f32(c_w_out[i])
        x = x + y.astype(x.dtype)
    return _rmsnorm(x, final_norm_g)
```

```python
import functools
import math

import numpy as np
import jax
import jax.numpy as jnp
from jax import lax
from jax.experimental import pallas as pl
from jax.experimental.pallas import tpu as pltpu

F32 = jnp.float32
BF16 = jnp.bfloat16

D_MODEL = 1024
CHUNK = 64
EPS = 1e-6
NEG = -1e30

SSD_HEADS = 16
SSD_HEAD_DIM = 64
SSD_INNER = 1024
SSD_GROUPS = 2
SSD_HPG = 8
SSD_STATE = 128
SSD_GN = 256
SSD_XBC = 1536
SSD_CONV = 4

ML_HEADS = 4
ML_QK_DIM = 128
ML_V_DIM = 256
ML_QK = 512
ML_INNER = 1024
ML_CONV = 4

DA_HEADS = 8
DA_QK_DIM = 64
DA_V_DIM = 128
DA_QK = 1024
DA_INNER = 1024

AB_SIZES = (SSD_INNER, SSD_XBC, SSD_HEADS, ML_INNER, ML_QK, ML_QK, ML_INNER, ML_HEADS, ML_HEADS, ML_INNER)
AB_OFFS = tuple(int(s) for s in np.cumsum((0,) + AB_SIZES))

LANES = 128
SUBLANES = 8

T_AB = 256
SM_DT = 0
SM_IG = 16
SM_FG = 20
VMEM_LIMIT = 56 * 1024 * 1024


def _split3(a):
    hi = a.astype(BF16)
    r = a - hi.astype(F32)
    mid = r.astype(BF16)
    lo = (r - mid.astype(F32)).astype(BF16)
    return hi, mid, lo


def _dot(a, b):
    return jnp.dot(a, b, preferred_element_type=F32)


def _dot_nt(a, b):
    return lax.dot_general(a, b, (((1,), (1,)), ((), ())), preferred_element_type=F32)


def _dot_tn(a, b):
    return lax.dot_general(a, b, (((0,), (0,)), ((), ())), preferred_element_type=F32)


def _sigmoid(v):
    return 1.0 / (1.0 + jnp.exp(-v))


def _silu(v):
    return v * _sigmoid(v)


def _ab_kernel(x_ref, g_ref, wza_ref, wxbc_ref, wzb_ref, wqk_ref, wv_ref, wog_ref, wsm_ref,
               scw_ref, scb_ref, smb_ref, sma_ref, dskip_ref, sng_ref,
               mcw_ref, mcb_ref, mng_ref, wout_ref, tril_ref, e16_ref,
               out_ref,
               xbcp, qkp, sst, cst, nst, mst, ybuf):
    T = T_AB
    c = pl.program_id(1)

    @pl.when(c == 0)
    def _():
        xbcp[0:SUBLANES, :] = jnp.zeros((SUBLANES, SSD_XBC), F32)
        qkp[0:SUBLANES, :] = jnp.zeros((SUBLANES, 2 * ML_QK), F32)
        sst[...] = jnp.zeros_like(sst)
        cst[...] = jnp.zeros_like(cst)
        nst[...] = jnp.zeros_like(nst)
        mst[...] = jnp.zeros_like(mst)

    x = x_ref[0]
    ms = jnp.mean(x * x, axis=-1, keepdims=True)
    h = (x * lax.rsqrt(ms + EPS) * g_ref[...]).astype(BF16)

    def conv(pad_ref, w_ref, cw_ref, cb_ref):
        pad_ref[SUBLANES:SUBLANES + T, :] = _dot(h, w_ref[...])
        acc = cb_ref[...] + cw_ref[3:4, :] * pad_ref[SUBLANES:SUBLANES + T, :]
        for k in range(3):
            off = SUBLANES - 3 + k
            acc = acc + cw_ref[k:k + 1, :] * pad_ref[off:off + T, :]
        pad_ref[0:SUBLANES, :] = pad_ref[T:T + SUBLANES, :]
        return _silu(acc)

    xbc = conv(xbcp, wxbc_ref, scw_ref, scb_ref)
    qk = conv(qkp, wqk_ref, mcw_ref, mcb_ref)

    lane = lax.broadcasted_iota(jnp.int32, (T, LANES), 1)
    sv = _dot(h, wsm_ref[...]) + smb_ref[...]
    e = jnp.log1p(jnp.exp(-jnp.abs(sv)))
    dt = jnp.where(lane < SM_IG, jnp.maximum(sv, 0.0) + e, 0.0)
    fgl = jnp.minimum(sv, 0.0) - e
    cs_in = jnp.where(lane < SM_IG, dt * sma_ref[...],
                      jnp.where(lane < SM_FG, 0.0, jnp.where(lane < SM_FG + ML_HEADS, fgl, 0.0)))
    tril = tril_ref[...]
    c_hi, c_mid, c_lo = _split3(cs_in)
    cs = _dot(tril, c_hi) + _dot(tril, c_mid) + _dot(tril, c_lo)
    scol = jnp.where(lane < SM_IG, cs, jnp.where(lane < SM_FG, sv, cs))
    srow = scol.T
    cs_end = cs[T - 1:T, :]
    eacs = jnp.where(lane < SM_IG, jnp.exp(cs), 0.0)
    toend = jnp.where(lane < SM_IG, jnp.exp(cs_end - cs), 0.0)

    e16 = e16_ref[...]

    def expand(a):
        return _dot(jnp.concatenate(_split3(a), axis=1), e16)

    dt_x = expand(dt)
    eacs_x = expand(eacs)
    toend_x = expand(toend)

    rowi = lax.broadcasted_iota(jnp.int32, (T, T), 0)
    coli = lax.broadcasted_iota(jnp.int32, (T, T), 1)
    causal = rowi >= coli
    lane64 = lane < SSD_HEAD_DIM

    xs = xbc[:, :SSD_INNER]
    xdt = xs * dt_x
    xdt_end = (xdt * toend_x).astype(BF16)
    eend_x = eacs_x[T - 1:T, :]
    za = _dot(h, wza_ref[...])
    gate_a = _silu(za)
    for g in range(SSD_GROUPS):
        bm = xbc[:, SSD_INNER + g * SSD_STATE:SSD_INNER + (g + 1) * SSD_STATE].astype(BF16)
        cm = xbc[:, SSD_INNER + SSD_GN + g * SSD_STATE:SSD_INNER + SSD_GN + (g + 1) * SSD_STATE].astype(BF16)
        cb = _dot_nt(cm, bm)
        gs = slice(g * 512, (g + 1) * 512)
        s_old = sst[g]
        yoff = _dot(cm, s_old.astype(BF16)) * eacs_x[:, gs]
        for p in range(SSD_HPG // 2):
            hd0 = g * SSD_HPG + 2 * p
            ms_ = []
            for hd in (hd0, hd0 + 1):
                seg = scol[:, hd:hd + 1] - srow[hd:hd + 1, :]
                dec = jnp.exp(jnp.where(causal, seg, NEG))
                ms_.append((cb * dec).astype(BF16))
            mcat = jnp.concatenate(ms_, axis=1)
            ls = slice(hd0 * SSD_HEAD_DIM, (hd0 + 2) * SSD_HEAD_DIM)
            slab = xdt[:, ls]
            bd = jnp.concatenate([jnp.where(lane64, slab, 0.0).astype(BF16),
                                  jnp.where(lane64, 0.0, slab).astype(BF16)], axis=0)
            y = _dot(mcat, bd) + yoff[:, 2 * p * SSD_HEAD_DIM:(2 * p + 2) * SSD_HEAD_DIM]
            y = y + xs[:, ls] * dskip_ref[:, ls]
            ybuf[:, ls] = y * gate_a[:, ls]
        sst[g] = s_old * eend_x[:, gs] + _dot_tn(bm, xdt_end[:, gs])
    for g in range(SSD_GROUPS):
        gs = slice(g * 512, (g + 1) * 512)
        yg = ybuf[:, gs]
        inv = lax.rsqrt(jnp.mean(yg * yg, axis=-1, keepdims=True) + EPS)
        ybuf[:, gs] = yg * inv * sng_ref[:, gs]

    vv = _dot(h, wv_ref[...])
    og = _dot(h, wog_ref[...])
    zb = _dot(h, wzb_ref[...])
    for hd in range(ML_HEADS):
        q_h = qk[:, hd * ML_QK_DIM:(hd + 1) * ML_QK_DIM]
        k_h = qk[:, ML_QK + hd * ML_QK_DIM:ML_QK + (hd + 1) * ML_QK_DIM] * (ML_QK_DIM ** -0.5)
        vs = slice(hd * ML_V_DIM, (hd + 1) * ML_V_DIM)
        v_h = vv[:, vs].astype(BF16)
        q_b = q_h.astype(BF16)
        b_c = scol[:, SM_FG + hd:SM_FG + hd + 1]
        b_r = srow[SM_FG + hd:SM_FG + hd + 1, :]
        ig_c = scol[:, SM_IG + hd:SM_IG + hd + 1]
        ig_r = srow[SM_IG + hd:SM_IG + hd + 1, :]
        mp = mst[hd:hd + 1, 0:1]
        c_old = cst[hd]
        n_old = nst[hd:hd + 1, :]
        dlog = jnp.where(causal, b_c - b_r + ig_r, NEG)
        inter = b_c + mp
        m_t = jnp.maximum(inter, jnp.max(dlog, axis=1, keepdims=True))
        w_intra = jnp.exp(dlog - m_t)
        w_inter = jnp.exp(inter - m_t)
        s = _dot_nt(q_b, k_h.astype(BF16)) * w_intra
        num = _dot(s.astype(BF16), v_h) + w_inter * _dot(q_b, c_old.astype(BF16))
        den = jnp.sum(s, axis=1, keepdims=True) + w_inter * jnp.sum(q_h * n_old, axis=1, keepdims=True)
        hh = num / jnp.maximum(jnp.abs(den), jnp.exp(-m_t))
        b_end = b_c[T - 1:T, :]
        glog = b_end - b_c + ig_c
        m_new = jnp.maximum(b_end + mp, jnp.max(glog, axis=0, keepdims=True))
        kg = k_h * jnp.exp(glog - m_new)
        dec = jnp.exp(b_end + mp - m_new)
        cst[hd] = dec * c_old + _dot_tn(kg.astype(BF16), v_h)
        nst[hd:hd + 1, :] = dec * n_old + jnp.sum(kg, axis=0, keepdims=True)
        mst[hd:hd + 1, :] = jnp.broadcast_to(m_new, (1, LANES))
        ho = hh * _sigmoid(og[:, vs])
        inv = lax.rsqrt(jnp.mean(ho * ho, axis=-1, keepdims=True) + EPS)
        ybuf[:, ML_INNER + hd * ML_V_DIM:ML_INNER + (hd + 1) * ML_V_DIM] = (
            ho * inv * mng_ref[:, vs] * _silu(zb[:, vs]))

    y = _dot(ybuf[...].astype(BF16), wout_ref[...])
    out_ref[0] = x + y


def _const_spec(shape):
    nd = len(shape)
    return pl.BlockSpec(shape, lambda b, c: (0,) * nd, pipeline_mode=pl.Buffered(1))


def _layer_ab(x, norm_g, w_in, ssd_cw, ssd_cb, dt_bias, a_log, d_skip, ssd_ng,
              ml_cw, ml_cb, i_bias, f_bias, ml_ng, w_out):
    B, L, D = x.shape
    T = T_AB
    o = AB_OFFS
    seg = lambda i: w_in[:, o[i]:o[i + 1]]
    wza, wxbc, wdt, wzb, wq, wk, wv, wig, wfg, wog = [seg(i) for i in range(10)]
    wsm = jnp.concatenate([wdt, wig, wfg, jnp.zeros((D, LANES - SM_FG - ML_HEADS), F32)], axis=1)
    wqk = jnp.concatenate([wq, wk], axis=1)
    pad = jnp.zeros((LANES - SM_FG - ML_HEADS,), F32)
    smb = jnp.concatenate([dt_bias, i_bias, f_bias, pad])[None, :]
    sma = jnp.concatenate([-jnp.exp(a_log), jnp.zeros((LANES - SSD_HEADS,), F32)])[None, :]
    dskip = jnp.repeat(d_skip, SSD_HEAD_DIM)[None, :]
    tril = jnp.asarray(np.tril(np.ones((T, T), np.float32)), BF16)
    e16_np = np.zeros((LANES, SSD_INNER), np.float32)
    for hd in range(SSD_HEADS):
        e16_np[hd, hd * SSD_HEAD_DIM:(hd + 1) * SSD_HEAD_DIM] = 1.0
    e16 = jnp.asarray(np.concatenate([e16_np] * 3, axis=0), BF16)
    bf = lambda a: a.astype(BF16)
    row = lambda a: a[None, :]
    args = (x, row(norm_g), bf(wza), bf(wxbc), bf(wzb), bf(wqk), bf(wv), bf(wog), bf(wsm),
            ssd_cw, row(ssd_cb), smb, sma, dskip, row(ssd_ng),
            ml_cw, row(ml_cb), row(ml_ng), bf(w_out), tril, e16)
    in_specs = [pl.BlockSpec((1, T, D), lambda b, c: (b, c, 0))]
    in_specs += [_const_spec(a.shape) for a in args[1:]]
    return pl.pallas_call(
        _ab_kernel,
        grid=(B, L // T),
        in_specs=in_specs,
        out_specs=pl.BlockSpec((1, T, D), lambda b, c: (b, c, 0)),
        out_shape=jax.ShapeDtypeStruct((B, L, D), F32),
        scratch_shapes=[
            pltpu.VMEM((T + SUBLANES, SSD_XBC), F32),
            pltpu.VMEM((T + SUBLANES, 2 * ML_QK), F32),
            pltpu.VMEM((SSD_GROUPS, SSD_STATE, SSD_HPG * SSD_HEAD_DIM), F32),
            pltpu.VMEM((ML_HEADS, ML_QK_DIM, ML_V_DIM), F32),
            pltpu.VMEM((SUBLANES, ML_QK_DIM), F32),
            pltpu.VMEM((SUBLANES, LANES), F32),
            pltpu.VMEM((T, SSD_INNER + ML_INNER), F32),
        ],
        compiler_params=pltpu.CompilerParams(
            dimension_semantics=("arbitrary", "arbitrary"), vmem_limit_bytes=VMEM_LIMIT),
        name="layer_ab",
    )(*args)


T_Q = 256
T_S = 256
T_KV = 1024
V_ROWS = DA_V_DIM + 16
LAYER_C = 1
LAMBDA_INIT = 0.8 - 0.6 * math.exp(-0.3 * LAYER_C)
ALIBI_SLOPES = tuple(2.0 ** (-8.0 * (h + 1) / DA_HEADS) for h in range(DA_HEADS))


def _cproj_kernel(x_ref, g_ref, wq_ref, wk_ref, wv_ref, wz_ref, q_ref, k_ref, vt_ref, z_ref):
    x = x_ref[0]
    ms = jnp.mean(x * x, axis=-1, keepdims=True)
    h = (x * lax.rsqrt(ms + EPS) * g_ref[...]).astype(BF16)
    q_ref[0] = _dot(h, wq_ref[...]).astype(BF16)
    z_ref[0] = _dot(h, wz_ref[...]).astype(BF16)
    kk = _dot(h, wk_ref[...])
    vv = _dot(h, wv_ref[...])
    tm = x.shape[0]
    for hd in range(DA_HEADS):
        ls = slice(hd * DA_V_DIM, (hd + 1) * DA_V_DIM)
        k_ref[0, hd] = kk[:, ls].astype(BF16)
        vt_ref[0, hd, 0, 0:DA_V_DIM, :] = vv[:, ls].T.astype(BF16)
        vt_ref[0, hd, 0, DA_V_DIM:V_ROWS, :] = jnp.ones((V_ROWS - DA_V_DIM, tm), BF16)


def _layer_c_proj(x, norm_g, w_in):
    B, L, D = x.shape
    tm = T_S
    wq, wk, wv, wz = [w_in[:, i * DA_QK:(i + 1) * DA_QK].astype(BF16) for i in range(4)]
    cspec = lambda shape: pl.BlockSpec(shape, lambda b, i: (0,) * len(shape), pipeline_mode=pl.Buffered(1))
    return pl.pallas_call(
        _cproj_kernel,
        grid=(B, L // tm),
        in_specs=[pl.BlockSpec((1, tm, D), lambda b, i: (b, i, 0)), cspec((1, D)),
                  cspec((D, DA_QK)), cspec((D, DA_QK)), cspec((D, DA_INNER)), cspec((D, DA_INNER))],
        out_specs=[pl.BlockSpec((1, tm, DA_QK), lambda b, i: (b, i, 0)),
                   pl.BlockSpec((1, DA_HEADS, tm, 2 * DA_QK_DIM), lambda b, i: (b, 0, i, 0)),
                   pl.BlockSpec((1, DA_HEADS, 1, V_ROWS, tm), lambda b, i: (b, 0, i, 0, 0)),
                   pl.BlockSpec((1, tm, DA_INNER), lambda b, i: (b, i, 0))],
        out_shape=[jax.ShapeDtypeStruct((B, L, DA_QK), BF16),
                   jax.ShapeDtypeStruct((B, DA_HEADS, L, 2 * DA_QK_DIM), BF16),
                   jax.ShapeDtypeStruct((B, DA_HEADS, L // tm, V_ROWS, tm), BF16),
                   jax.ShapeDtypeStruct((B, L, DA_INNER), BF16)],
        compiler_params=pltpu.CompilerParams(
            dimension_semantics=("arbitrary", "arbitrary"), vmem_limit_bytes=VMEM_LIMIT),
        name="layer_c_proj",
    )(x, norm_g[None, :], wq, wk, wv, wz)


def _attn_kernel(q_ref, k_ref, vt_ref, z_ref, x_ref, bias_ref, slope_ref, lamv_ref, subg_ref, fng_ref, wout_ref,
                 out_ref, qt_scr, m_scr, acc_scr, ybuf):
    i = pl.program_id(1)
    j = pl.program_id(2)
    nkv = pl.num_programs(2)
    q0 = i * T_Q
    k_base = j * T_KV
    n_heads2 = 2 * DA_HEADS

    @pl.when(j == 0)
    def _():
        m_scr[...] = jnp.full(m_scr.shape, NEG, F32)
        acc_scr[...] = jnp.zeros_like(acc_scr)
        rowi = lax.broadcasted_iota(jnp.int32, (2 * DA_QK_DIM, T_Q), 0)
        for hd in range(DA_HEADS):
            qh = q_ref[0, :, hd * 2 * DA_QK_DIM:(hd + 1) * 2 * DA_QK_DIM].astype(F32) * (DA_QK_DIM ** -0.5)
            qht = qh.T
            qt_scr[2 * hd] = jnp.where(rowi < DA_QK_DIM, qht, 0.0).astype(BF16)
            qt_scr[2 * hd + 1] = jnp.where(rowi < DA_QK_DIM, 0.0, qht).astype(BF16)

    @pl.when(k_base <= q0)
    def _():
        n_sub = jnp.minimum(T_KV // T_S, (q0 - k_base) // T_S + 1)

        def head_body(hd, carry):
            slope = slope_ref[hd]

            def sub_body(s, carry2):
                k0 = k_base + s * T_S
                kb = k_ref[0, hd, pl.ds(pl.multiple_of(s * T_S, T_S), T_S), :]
                vb = vt_ref[0, hd, s]
                diag = (k0 == q0).astype(jnp.int32)
                bias = bias_ref[diag, hd]
                coff = slope * jnp.full((1, T_Q), k0 - q0, jnp.int32).astype(F32)
                for half in range(2):
                    hh = 2 * hd + half
                    st = _dot(kb, qt_scr[hh]) + bias
                    m_old = m_scr[hh]
                    m_new = jnp.maximum(m_old, jnp.max(st, axis=0, keepdims=True) + coff)
                    alpha = jnp.exp(m_old - m_new)
                    pt = jnp.exp(st - (m_new - coff)).astype(BF16)
                    acc_scr[hh] = acc_scr[hh] * alpha + _dot(vb, pt)
                    m_scr[hh] = m_new
                return carry2

            lax.fori_loop(0, n_sub, sub_body, 0)
            return carry

        lax.fori_loop(0, DA_HEADS, head_body, 0)

    @pl.when(j == nkv - 1)
    def _():
        lv = lamv_ref[...]
        lam = (jnp.exp(jnp.sum(lv[0:1] * lv[1:2], axis=1, keepdims=True))
               - jnp.exp(jnp.sum(lv[2:3] * lv[3:4], axis=1, keepdims=True)) + LAMBDA_INIT)
        for hd in range(DA_HEADS):
            a1 = acc_scr[2 * hd]
            a2 = acc_scr[2 * hd + 1]
            o1 = a1[0:DA_V_DIM] / a1[DA_V_DIM:DA_V_DIM + 1]
            o2 = a2[0:DA_V_DIM] / a2[DA_V_DIM:DA_V_DIM + 1]
            ot = o1 - lam * o2
            inv = lax.rsqrt(jnp.mean(ot * ot, axis=0, keepdims=True) + EPS)
            ot = ot * inv * subg_ref[...] * (1.0 - LAMBDA_INIT)
            ls = slice(hd * DA_V_DIM, (hd + 1) * DA_V_DIM)
            ybuf[:, ls] = ot.T * _silu(z_ref[0, :, ls].astype(F32))
        y = _dot(ybuf[...].astype(BF16), wout_ref[...])
        xo = x_ref[0] + y
        inv = lax.rsqrt(jnp.mean(xo * xo, axis=-1, keepdims=True) + EPS)
        out_ref[0] = xo * inv * fng_ref[...]


def _attn_bias():
    jj = np.arange(T_S, dtype=np.float64)[:, None]
    ii = np.arange(T_Q, dtype=np.float64)[None, :]
    out = np.zeros((2, DA_HEADS, T_S, T_Q), np.float32)
    allowed = (jj // CHUNK) <= (ii // CHUNK)
    for hd, sl in enumerate(ALIBI_SLOPES):
        out[0, hd] = sl * jj
        out[1, hd] = np.where(allowed, sl * (ii - np.abs(ii - jj)), NEG)
    return jnp.asarray(out)


def _layer_c_attn(x, q, k, vt, z, lam_q1, lam_k1, lam_q2, lam_k2, subln_g, w_out, final_g):
    B, L, D = x.shape
    nq, nkv = L // T_Q, L // T_KV
    lamv = jnp.zeros((SUBLANES, LANES), F32)
    for r, v in enumerate((lam_q1, lam_k1, lam_q2, lam_k2)):
        lamv = lamv.at[r, :DA_QK_DIM].set(v)
    slopes = jnp.asarray(np.broadcast_to(np.asarray(ALIBI_SLOPES, np.float32)[:, None, None], (DA_HEADS, 1, T_Q)))
    kvmax = lambda i: (i * T_Q) // T_KV
    cspec = lambda shape: pl.BlockSpec(shape, lambda b, i, j: (0,) * len(shape), pipeline_mode=pl.Buffered(1))
    row_spec = pl.BlockSpec((1, T_Q, D), lambda b, i, j: (b, i, 0))
    return pl.pallas_call(
        _attn_kernel,
        grid=(B, nq, nkv),
        in_specs=[row_spec,
                  pl.BlockSpec((1, DA_HEADS, T_KV, 2 * DA_QK_DIM), lambda b, i, j: (b, 0, jnp.minimum(j, kvmax(i)), 0)),
                  pl.BlockSpec((1, DA_HEADS, T_KV // T_S, V_ROWS, T_S),
                               lambda b, i, j: (b, 0, jnp.minimum(j, kvmax(i)), 0, 0)),
                  row_spec, row_spec,
                  cspec((2, DA_HEADS, T_S, T_Q)), cspec((DA_HEADS, 1, T_Q)), cspec((SUBLANES, LANES)),
                  cspec((DA_V_DIM, 1)),
                  cspec((1, D)), cspec((DA_INNER, D))],
        out_specs=row_spec,
        out_shape=jax.ShapeDtypeStruct((B, L, D), F32),
        scratch_shapes=[
            pltpu.VMEM((2 * DA_HEADS, 2 * DA_QK_DIM, T_Q), BF16),
            pltpu.VMEM((2 * DA_HEADS, 1, T_Q), F32),
            pltpu.VMEM((2 * DA_HEADS, V_ROWS, T_Q), F32),
            pltpu.VMEM((T_Q, DA_INNER), F32),
        ],
        compiler_params=pltpu.CompilerParams(
            dimension_semantics=("arbitrary", "arbitrary", "arbitrary"), vmem_limit_bytes=VMEM_LIMIT),
        name="layer_c_attn",
    )(q, k, vt, z, x, _attn_bias(), slopes, lamv, subln_g[:, None], final_g[None, :], w_out.astype(BF16))


def kernel(x, ab_norm_g, ab_w_in, ab_ssd_conv_w, ab_ssd_conv_b, ab_dt_bias, ab_a_log, ab_d_skip, ab_ssd_norm_g,
           ab_ml_conv_w, ab_ml_conv_b, ab_i_bias, ab_f_bias, ab_ml_norm_g, ab_w_out,
           c_norm_g, c_w_in, c_lam_q1, c_lam_k1, c_lam_q2, c_lam_k2, c_subln_g, c_w_out, final_norm_g):
    x1 = _layer_ab(x, ab_norm_g[0], ab_w_in[0], ab_ssd_conv_w[0], ab_ssd_conv_b[0], ab_dt_bias[0], ab_a_log[0],
                   ab_d_skip[0], ab_ssd_norm_g[0], ab_ml_conv_w[0], ab_ml_conv_b[0], ab_i_bias[0], ab_f_bias[0],
                   ab_ml_norm_g[0], ab_w_out[0])
    q, k, vt, z = _layer_c_proj(x1, c_norm_g[0], c_w_in[0])
    return _layer_c_attn(x1, q, k, vt, z, c_lam_q1[0], c_lam_k1[0], c_lam_q2[0], c_lam_k2[0],
                         c_subln_g[0], c_w_out[0], final_norm_g)
```

```python
import functools
import math

import numpy as np
import jax
import jax.numpy as jnp
from jax import lax
from jax.experimental import pallas as pl
from jax.experimental.pallas import tpu as pltpu

F32 = jnp.float32
BF16 = jnp.bfloat16

D_MODEL = 1024
CHUNK = 64
EPS = 1e-6
NEG = -1e30

SSD_HEADS = 16
SSD_HEAD_DIM = 64
SSD_INNER = 1024
SSD_GROUPS = 2
SSD_HPG = 8
SSD_STATE = 128
SSD_GN = 256
SSD_XBC = 1536
SSD_CONV = 4

ML_HEADS = 4
ML_QK_DIM = 128
ML_V_DIM = 256
ML_QK = 512
ML_INNER = 1024
ML_CONV = 4

DA_HEADS = 8
DA_QK_DIM = 64
DA_V_DIM = 128
DA_QK = 1024
DA_INNER = 1024

AB_SIZES = (SSD_INNER, SSD_XBC, SSD_HEADS, ML_INNER, ML_QK, ML_QK, ML_INNER, ML_HEADS, ML_HEADS, ML_INNER)
AB_OFFS = tuple(int(s) for s in np.cumsum((0,) + AB_SIZES))

LANES = 128
SUBLANES = 8

T_AB = 256
SM_DT = 0
SM_IG = 16
SM_FG = 20
VMEM_LIMIT = 56 * 1024 * 1024


def _split3(a):
    hi = a.astype(BF16)
    r = a - hi.astype(F32)
    mid = r.astype(BF16)
    lo = (r - mid.astype(F32)).astype(BF16)
    return hi, mid, lo


def _dot(a, b):
    return jnp.dot(a, b, preferred_element_type=F32)


def _dot_nt(a, b):
    return lax.dot_general(a, b, (((1,), (1,)), ((), ())), preferred_element_type=F32)


def _dot_tn(a, b):
    return lax.dot_general(a, b, (((0,), (0,)), ((), ())), preferred_element_type=F32)


def _sigmoid(v):
    return 1.0 / (1.0 + jnp.exp(-v))


def _silu(v):
    return v * _sigmoid(v)


def _ab_kernel(x_ref, g_ref, wza_ref, wxbc_ref, wzb_ref, wqk_ref, wv_ref, wog_ref, wsm_ref,
               scw_ref, scb_ref, smb_ref, sma_ref, dskip_ref, sng_ref,
               mcw_ref, mcb_ref, mng_ref, wout_ref, tril_ref, e16_ref,
               out_ref,
               xbcp, qkp, sst, cst, nst, mst, ybuf):
    T = T_AB
    c = pl.program_id(1)

    @pl.when(c == 0)
    def _():
        xbcp[0:SUBLANES, :] = jnp.zeros((SUBLANES, SSD_XBC), F32)
        qkp[0:SUBLANES, :] = jnp.zeros((SUBLANES, 2 * ML_QK), F32)
        sst[...] = jnp.zeros_like(sst)
        cst[...] = jnp.zeros_like(cst)
        nst[...] = jnp.zeros_like(nst)
        mst[...] = jnp.zeros_like(mst)

    x = x_ref[0]
    ms = jnp.mean(x * x, axis=-1, keepdims=True)
    h = (x * lax.rsqrt(ms + EPS) * g_ref[...]).astype(BF16)

    def conv(pad_ref, w_ref, cw_ref, cb_ref):
        pad_ref[SUBLANES:SUBLANES + T, :] = _dot(h, w_ref[...])
        acc = cb_ref[...] + cw_ref[3:4, :] * pad_ref[SUBLANES:SUBLANES + T, :]
        for k in range(3):
            off = SUBLANES - 3 + k
            acc = acc + cw_ref[k:k + 1, :] * pad_ref[off:off + T, :]
        pad_ref[0:SUBLANES, :] = pad_ref[T:T + SUBLANES, :]
        return _silu(acc)

    xbc = conv(xbcp, wxbc_ref, scw_ref, scb_ref)
    qk = conv(qkp, wqk_ref, mcw_ref, mcb_ref)

    lane = lax.broadcasted_iota(jnp.int32, (T, LANES), 1)
    sv = _dot(h, wsm_ref[...]) + smb_ref[...]
    e = jnp.log1p(jnp.exp(-jnp.abs(sv)))
    dt = jnp.where(lane < SM_IG, jnp.maximum(sv, 0.0) + e, 0.0)
    fgl = jnp.minimum(sv, 0.0) - e
    cs_in = jnp.where(lane < SM_IG, dt * sma_ref[...],
                      jnp.where(lane < SM_FG, 0.0, jnp.where(lane < SM_FG + ML_HEADS, fgl, 0.0)))
    tril = tril_ref[...]
    c_hi, c_mid, c_lo = _split3(cs_in)
    cs = _dot(tril, c_hi) + _dot(tril, c_mid) + _dot(tril, c_lo)
    scol = jnp.where(lane < SM_IG, cs, jnp.where(lane < SM_FG, sv, cs))
    srow = scol.T
    cs_end = cs[T - 1:T, :]
    eacs = jnp.where(lane < SM_IG, jnp.exp(cs), 0.0)
    toend = jnp.where(lane < SM_IG, jnp.exp(cs_end - cs), 0.0)

    e16 = e16_ref[...]

    def expand(a):
        return _dot(jnp.concatenate(_split3(a), axis=1), e16)

    dt_x = expand(dt)
    eacs_x = expand(eacs)
    toend_x = expand(toend)

    rowi = lax.broadcasted_iota(jnp.int32, (T, T), 0)
    coli = lax.broadcasted_iota(jnp.int32, (T, T), 1)
    causal = rowi >= coli
    lane64 = lane < SSD_HEAD_DIM

    xs = xbc[:, :SSD_INNER]
    xdt = xs * dt_x
    xdt_end = (xdt * toend_x).astype(BF16)
    eend_x = eacs_x[T - 1:T, :]
    za = _dot(h, wza_ref[...])
    gate_a = _silu(za)
    for g in range(SSD_GROUPS):
        bm = xbc[:, SSD_INNER + g * SSD_STATE:SSD_INNER + (g + 1) * SSD_STATE].astype(BF16)
        cm = xbc[:, SSD_INNER + SSD_GN + g * SSD_STATE:SSD_INNER + SSD_GN + (g + 1) * SSD_STATE].astype(BF16)
        cb = _dot_nt(cm, bm)
        gs = slice(g * 512, (g + 1) * 512)
        s_old = sst[g]
        yoff = _dot(cm, s_old.astype(BF16)) * eacs_x[:, gs]
        for p in range(SSD_HPG // 2):
            hd0 = g * SSD_HPG + 2 * p
            ms_ = []
            for hd in (hd0, hd0 + 1):
                seg = scol[:, hd:hd + 1] - srow[hd:hd + 1, :]
                dec = jnp.exp(jnp.where(causal, seg, NEG))
                ms_.append((cb * dec).astype(BF16))
            mcat = jnp.concatenate(ms_, axis=1)
            ls = slice(hd0 * SSD_HEAD_DIM, (hd0 + 2) * SSD_HEAD_DIM)
            slab = xdt[:, ls]
            bd = jnp.concatenate([jnp.where(lane64, slab, 0.0).astype(BF16),
                                  jnp.where(lane64, 0.0, slab).astype(BF16)], axis=0)
            y = _dot(mcat, bd) + yoff[:, 2 * p * SSD_HEAD_DIM:(2 * p + 2) * SSD_HEAD_DIM]
            y = y + xs[:, ls] * dskip_ref[:, ls]
            ybuf[:, ls] = y * gate_a[:, ls]
        sst[g] = s_old * eend_x[:, gs] + _dot_tn(bm, xdt_end[:, gs])
    for g in range(SSD_GROUPS):
        gs = slice(g * 512, (g + 1) * 512)
        yg = ybuf[:, gs]
        inv = lax.rsqrt(jnp.mean(yg * yg, axis=-1, keepdims=True) + EPS)
        ybuf[:, gs] = yg * inv * sng_ref[:, gs]

    vv = _dot(h, wv_ref[...])
    og = _dot(h, wog_ref[...])
    zb = _dot(h, wzb_ref[...])
    for hd in range(ML_HEADS):
        q_h = qk[:, hd * ML_QK_DIM:(hd + 1) * ML_QK_DIM]
        k_h = qk[:, ML_QK + hd * ML_QK_DIM:ML_QK + (hd + 1) * ML_QK_DIM] * (ML_QK_DIM ** -0.5)
        vs = slice(hd * ML_V_DIM, (hd + 1) * ML_V_DIM)
        v_h = vv[:, vs].astype(BF16)
        q_b = q_h.astype(BF16)
        b_c = scol[:, SM_FG + hd:SM_FG + hd + 1]
        b_r = srow[SM_FG + hd:SM_FG + hd + 1, :]
        ig_c = scol[:, SM_IG + hd:SM_IG + hd + 1]
        ig_r = srow[SM_IG + hd:SM_IG + hd + 1, :]
        mp = mst[hd:hd + 1, 0:1]
        c_old = cst[hd]
        n_old = nst[hd:hd + 1, :]
        dlog = jnp.where(causal, b_c - b_r + ig_r, NEG)
        inter = b_c + mp
        m_t = jnp.maximum(inter, jnp.max(dlog, axis=1, keepdims=True))
        w_intra = jnp.exp(dlog - m_t)
        w_inter = jnp.exp(inter - m_t)
        s = _dot_nt(q_b, k_h.astype(BF16)) * w_intra
        num = _dot(s.astype(BF16), v_h) + w_inter * _dot(q_b, c_old.astype(BF16))
        den = jnp.sum(s, axis=1, keepdims=True) + w_inter * jnp.sum(q_h * n_old, axis=1, keepdims=True)
        hh = num / jnp.maximum(jnp.abs(den), jnp.exp(-m_t))
        b_end = b_c[T - 1:T, :]
        glog = b_end - b_c + ig_c
        m_new = jnp.maximum(b_end + mp, jnp.max(glog, axis=0, keepdims=True))
        kg = k_h * jnp.exp(glog - m_new)
        dec = jnp.exp(b_end + mp - m_new)
        cst[hd] = dec * c_old + _dot_tn(kg.astype(BF16), v_h)
        nst[hd:hd + 1, :] = dec * n_old + jnp.sum(kg, axis=0, keepdims=True)
        mst[hd:hd + 1, :] = jnp.broadcast_to(m_new, (1, LANES))
        ho = hh * _sigmoid(og[:, vs])
        inv = lax.rsqrt(jnp.mean(ho * ho, axis=-1, keepdims=True) + EPS)
        ybuf[:, ML_INNER + hd * ML_V_DIM:ML_INNER + (hd + 1) * ML_V_DIM] = (
            ho * inv * mng_ref[:, vs] * _silu(zb[:, vs]))

    y = _dot(ybuf[...].astype(BF16), wout_ref[...])
    out_ref[0] = x + y


def _const_spec(shape):
    nd = len(shape)
    return pl.BlockSpec(shape, lambda b, c: (0,) * nd, pipeline_mode=pl.Buffered(1))


def _layer_ab(x, norm_g, w_in, ssd_cw, ssd_cb, dt_bias, a_log, d_skip, ssd_ng,
              ml_cw, ml_cb, i_bias, f_bias, ml_ng, w_out):
    B, L, D = x.shape
    T = T_AB
    o = AB_OFFS
    seg = lambda i: w_in[:, o[i]:o[i + 1]]
    wza, wxbc, wdt, wzb, wq, wk, wv, wig, wfg, wog = [seg(i) for i in range(10)]
    wsm = jnp.concatenate([wdt, wig, wfg, jnp.zeros((D, LANES - SM_FG - ML_HEADS), F32)], axis=1)
    wqk = jnp.concatenate([wq, wk], axis=1)
    pad = jnp.zeros((LANES - SM_FG - ML_HEADS,), F32)
    smb = jnp.concatenate([dt_bias, i_bias, f_bias, pad])[None, :]
    sma = jnp.concatenate([-jnp.exp(a_log), jnp.zeros((LANES - SSD_HEADS,), F32)])[None, :]
    dskip = jnp.repeat(d_skip, SSD_HEAD_DIM)[None, :]
    tril = jnp.asarray(np.tril(np.ones((T, T), np.float32)), BF16)
    e16_np = np.zeros((LANES, SSD_INNER), np.float32)
    for hd in range(SSD_HEADS):
        e16_np[hd, hd * SSD_HEAD_DIM:(hd + 1) * SSD_HEAD_DIM] = 1.0
    e16 = jnp.asarray(np.concatenate([e16_np] * 3, axis=0), BF16)
    bf = lambda a: a.astype(BF16)
    row = lambda a: a[None, :]
    args = (x, row(norm_g), bf(wza), bf(wxbc), bf(wzb), bf(wqk), bf(wv), bf(wog), bf(wsm),
            ssd_cw, row(ssd_cb), smb, sma, dskip, row(ssd_ng),
            ml_cw, row(ml_cb), row(ml_ng), bf(w_out), tril, e16)
    in_specs = [pl.BlockSpec((1, T, D), lambda b, c: (b, c, 0))]
    in_specs += [_const_spec(a.shape) for a in args[1:]]
    return pl.pallas_call(
        _ab_kernel,
        grid=(B, L // T),
        in_specs=in_specs,
        out_specs=pl.BlockSpec((1, T, D), lambda b, c: (b, c, 0)),
        out_shape=jax.ShapeDtypeStruct((B, L, D), F32),
        scratch_shapes=[
            pltpu.VMEM((T + SUBLANES, SSD_XBC), F32),
            pltpu.VMEM((T + SUBLANES, 2 * ML_QK), F32),
            pltpu.VMEM((SSD_GROUPS, SSD_STATE, SSD_HPG * SSD_HEAD_DIM), F32),
            pltpu.VMEM((ML_HEADS, ML_QK_DIM, ML_V_DIM), F32),
            pltpu.VMEM((SUBLANES, ML_QK_DIM), F32),
            pltpu.VMEM((SUBLANES, LANES), F32),
            pltpu.VMEM((T, SSD_INNER + ML_INNER), F32),
        ],
        compiler_params=pltpu.CompilerParams(
            dimension_semantics=("arbitrary", "arbitrary"), vmem_limit_bytes=VMEM_LIMIT),
        name="layer_ab",
    )(*args)


T_Q = 256
T_S = 256
T_KV = 1024
V_ROWS = DA_V_DIM + 16
LAYER_C = 1
LAMBDA_INIT = 0.8 - 0.6 * math.exp(-0.3 * LAYER_C)
ALIBI_SLOPES = tuple(2.0 ** (-8.0 * (h + 1) / DA_HEADS) for h in range(DA_HEADS))
LOG2E = 1.4426950408889634


def _cproj_kernel(x_ref, g_ref, wq_ref, wk_ref, wv_ref, wz_ref, q_ref, k_ref, vt_ref, z_ref):
    x = x_ref[0]
    ms = jnp.mean(x * x, axis=-1, keepdims=True)
    h = (x * lax.rsqrt(ms + EPS) * g_ref[...]).astype(BF16)
    q_ref[0] = _dot(h, wq_ref[...]).astype(BF16)
    z_ref[0] = _dot(h, wz_ref[...]).astype(BF16)
    kk = _dot(h, wk_ref[...])
    vv = _dot(h, wv_ref[...])
    tm = x.shape[0]
    for hd in range(DA_HEADS):
        ls = slice(hd * DA_V_DIM, (hd + 1) * DA_V_DIM)
        k_ref[0, hd] = kk[:, ls].astype(BF16)
        vt_ref[0, hd, 0, 0:DA_V_DIM, :] = vv[:, ls].T.astype(BF16)
        vt_ref[0, hd, 0, DA_V_DIM:V_ROWS, :] = jnp.ones((V_ROWS - DA_V_DIM, tm), BF16)


def _layer_c_proj(x, norm_g, w_in):
    B, L, D = x.shape
    tm = T_S
    wq, wk, wv, wz = [w_in[:, i * DA_QK:(i + 1) * DA_QK].astype(BF16) for i in range(4)]
    cspec = lambda shape: pl.BlockSpec(shape, lambda b, i: (0,) * len(shape), pipeline_mode=pl.Buffered(1))
    return pl.pallas_call(
        _cproj_kernel,
        grid=(B, L // tm),
        in_specs=[pl.BlockSpec((1, tm, D), lambda b, i: (b, i, 0)), cspec((1, D)),
                  cspec((D, DA_QK)), cspec((D, DA_QK)), cspec((D, DA_INNER)), cspec((D, DA_INNER))],
        out_specs=[pl.BlockSpec((1, tm, DA_QK), lambda b, i: (b, i, 0)),
                   pl.BlockSpec((1, DA_HEADS, tm, 2 * DA_QK_DIM), lambda b, i: (b, 0, i, 0)),
                   pl.BlockSpec((1, DA_HEADS, 1, V_ROWS, tm), lambda b, i: (b, 0, i, 0, 0)),
                   pl.BlockSpec((1, tm, DA_INNER), lambda b, i: (b, i, 0))],
        out_shape=[jax.ShapeDtypeStruct((B, L, DA_QK), BF16),
                   jax.ShapeDtypeStruct((B, DA_HEADS, L, 2 * DA_QK_DIM), BF16),
                   jax.ShapeDtypeStruct((B, DA_HEADS, L // tm, V_ROWS, tm), BF16),
                   jax.ShapeDtypeStruct((B, L, DA_INNER), BF16)],
        compiler_params=pltpu.CompilerParams(
            dimension_semantics=("arbitrary", "arbitrary"), vmem_limit_bytes=VMEM_LIMIT),
        name="layer_c_proj",
    )(x, norm_g[None, :], wq, wk, wv, wz)


def _attn_kernel(q_ref, k_ref, vt_ref, z_ref, x_ref, kaug_ref, qaug_ref, dbias_ref, cvec_ref, lamv_ref, subg_ref,
                 fng_ref, wout_ref, out_ref, qt_scr, m_scr, acc_scr, st_scr, pt_scr, ybuf):
    i = pl.program_id(1)
    j = pl.program_id(2)
    nkv = pl.num_programs(2)
    q0 = i * T_Q
    k_base = j * T_KV
    qk_rows = 2 * DA_QK_DIM

    @pl.when(j == 0)
    def _():
        m_scr[...] = jnp.full(m_scr.shape, NEG, F32)
        acc_scr[...] = jnp.zeros_like(acc_scr)
        rowi = lax.broadcasted_iota(jnp.int32, (qk_rows, T_Q), 0)
        for hd in range(DA_HEADS):
            qh = q_ref[0, :, hd * qk_rows:(hd + 1) * qk_rows].astype(F32) * (DA_QK_DIM ** -0.5 * LOG2E)
            qht = qh.T
            qt_scr[2 * hd, 0:qk_rows, :] = jnp.where(rowi < DA_QK_DIM, qht, 0.0).astype(BF16)
            qt_scr[2 * hd + 1, 0:qk_rows, :] = jnp.where(rowi < DA_QK_DIM, 0.0, qht).astype(BF16)
            qt_scr[2 * hd, qk_rows:, :] = qaug_ref[hd]
            qt_scr[2 * hd + 1, qk_rows:, :] = qaug_ref[hd]

    def run_chains(scores, s, coffs):
        n = 2 * DA_HEADS
        n_st, n_pt = st_scr.shape[0], pt_scr.shape[0]
        cmax, alpha = {}, {}

        def stage_a(hh):
            st = scores(hh)
            st_scr[hh % n_st] = st
            cmax[hh] = jnp.max(st, axis=0, keepdims=True)

        def stage_b(hh):
            coff = coffs[hh]
            m_old = m_scr[hh]
            m_new = jnp.maximum(m_old, cmax.pop(hh) + coff)
            alpha[hh] = jnp.exp2(m_old - m_new)
            pt_scr[hh % n_pt] = jnp.exp2(st_scr[hh % n_st] - (m_new - coff)).astype(BF16)
            m_scr[hh] = m_new

        def stage_c(hh):
            acc_scr[hh] = acc_scr[hh] * alpha.pop(hh) + _dot(vt_ref[0, hh // 2, s], pt_scr[hh % n_pt])

        stage_a(0)
        stage_a(1)
        stage_b(0)
        for hh in range(n):
            if hh + 2 < n:
                stage_a(hh + 2)
            if hh + 1 < n:
                stage_b(hh + 1)
            stage_c(hh)

    @pl.when(k_base <= q0)
    def _():
        n_full = jnp.minimum(T_KV // T_S, (q0 - k_base) // T_S)
        kaug = kaug_ref[...]

        def sub_body(s, carry):
            dk = jnp.full((1, T_Q), k_base + s * T_S - q0, jnp.int32).astype(F32)
            rows = pl.ds(pl.multiple_of(s * T_S, T_S), T_S)

            def scores(hh):
                lhs = jnp.concatenate([k_ref[0, hh // 2, rows, :], kaug], axis=1)
                return _dot(lhs, qt_scr[hh])

            coffs = [cvec_ref[hh // 2] * dk for hh in range(2 * DA_HEADS)]
            run_chains(scores, s, coffs)
            return carry

        lax.fori_loop(0, n_full, sub_body, 0)

        @pl.when(q0 - k_base < T_KV)
        def _():
            sd = (q0 - k_base) // T_S
            rows = pl.ds(pl.multiple_of(sd * T_S, T_S), T_S)

            def scores(hh):
                return _dot(k_ref[0, hh // 2, rows, :], qt_scr[hh, 0:qk_rows, :]) + dbias_ref[hh // 2]

            run_chains(scores, sd, [0.0] * (2 * DA_HEADS))

    @pl.when(j == nkv - 1)
    def _():
        lv = lamv_ref[...]
        lam = (jnp.exp(jnp.sum(lv[0:1] * lv[1:2], axis=1, keepdims=True))
               - jnp.exp(jnp.sum(lv[2:3] * lv[3:4], axis=1, keepdims=True)) + LAMBDA_INIT)
        for hd in range(DA_HEADS):
            a1 = acc_scr[2 * hd]
            a2 = acc_scr[2 * hd + 1]
            o1 = a1[0:DA_V_DIM] / a1[DA_V_DIM:DA_V_DIM + 1]
            o2 = a2[0:DA_V_DIM] / a2[DA_V_DIM:DA_V_DIM + 1]
            ot = o1 - lam * o2
            inv = lax.rsqrt(jnp.mean(ot * ot, axis=0, keepdims=True) + EPS)
            ot = ot * inv * subg_ref[...] * (1.0 - LAMBDA_INIT)
            ls = slice(hd * DA_V_DIM, (hd + 1) * DA_V_DIM)
            ybuf[:, ls] = ot.T * _silu(z_ref[0, :, ls].astype(F32))
        y = _dot(ybuf[...].astype(BF16), wout_ref[...])
        xo = x_ref[0] + y
        inv = lax.rsqrt(jnp.mean(xo * xo, axis=-1, keepdims=True) + EPS)
        out_ref[0] = xo * inv * fng_ref[...]


def _np_split3(c):
    import ml_dtypes
    rb = lambda a: a.astype(ml_dtypes.bfloat16).astype(np.float32)
    hi = rb(c)
    mid = rb(c - hi)
    lo = rb(c - hi - mid)
    return hi, mid, lo


def _attn_tables():
    c32 = (np.asarray(ALIBI_SLOPES, np.float64) * LOG2E).astype(np.float32)
    jj = np.arange(T_S, dtype=np.float32)[:, None]
    ii = np.arange(T_Q, dtype=np.float32)[None, :]
    allowed = (jj // CHUNK) <= (ii // CHUNK)
    kaug = np.zeros((T_S, LANES), np.float32)
    kaug[:, 0:3] = jj
    qaug = np.zeros((DA_HEADS, LANES, T_Q), np.float32)
    for r, part in enumerate(_np_split3(c32)):
        qaug[:, r, :] = part[:, None]
    dbias = np.where(allowed[None], c32[:, None, None] * (ii - np.abs(ii - jj))[None], np.float32(NEG))
    cvec = np.broadcast_to(c32[:, None, None], (DA_HEADS, 1, T_Q))
    return (jnp.asarray(kaug, BF16), jnp.asarray(qaug, BF16), jnp.asarray(dbias, F32), jnp.asarray(cvec, F32))


def _layer_c_attn(x, q, k, vt, z, lam_q1, lam_k1, lam_q2, lam_k2, subln_g, w_out, final_g):
    B, L, D = x.shape
    nq, nkv = L // T_Q, L // T_KV
    lamv = jnp.zeros((SUBLANES, LANES), F32)
    for r, v in enumerate((lam_q1, lam_k1, lam_q2, lam_k2)):
        lamv = lamv.at[r, :DA_QK_DIM].set(v)
    kaug, qaug, dbias, cvec = _attn_tables()
    kvmax = lambda i: (i * T_Q) // T_KV
    cspec = lambda shape: pl.BlockSpec(shape, lambda b, i, j: (0,) * len(shape), pipeline_mode=pl.Buffered(1))
    row_spec = pl.BlockSpec((1, T_Q, D), lambda b, i, j: (b, i, 0))
    return pl.pallas_call(
        _attn_kernel,
        grid=(B, nq, nkv),
        in_specs=[row_spec,
                  pl.BlockSpec((1, DA_HEADS, T_KV, 2 * DA_QK_DIM), lambda b, i, j: (b, 0, jnp.minimum(j, kvmax(i)), 0)),
                  pl.BlockSpec((1, DA_HEADS, T_KV // T_S, V_ROWS, T_S),
                               lambda b, i, j: (b, 0, jnp.minimum(j, kvmax(i)), 0, 0)),
                  row_spec, row_spec,
                  cspec((T_S, LANES)), cspec((DA_HEADS, LANES, T_Q)), cspec((DA_HEADS, T_S, T_Q)),
                  cspec((DA_HEADS, 1, T_Q)), cspec((SUBLANES, LANES)), cspec((DA_V_DIM, 1)),
                  cspec((1, D)), cspec((DA_INNER, D))],
        out_specs=row_spec,
        out_shape=jax.ShapeDtypeStruct((B, L, D), F32),
        scratch_shapes=[
            pltpu.VMEM((2 * DA_HEADS, 2 * DA_QK_DIM + LANES, T_Q), BF16),
            pltpu.VMEM((2 * DA_HEADS, 1, T_Q), F32),
            pltpu.VMEM((2 * DA_HEADS, V_ROWS, T_Q), F32),
            pltpu.VMEM((3, T_S, T_Q), F32),
            pltpu.VMEM((2, T_S, T_Q), BF16),
            pltpu.VMEM((T_Q, DA_INNER), F32),
        ],
        compiler_params=pltpu.CompilerParams(
            dimension_semantics=("arbitrary", "arbitrary", "arbitrary"), vmem_limit_bytes=VMEM_LIMIT),
        name="layer_c_attn",
    )(q, k, vt, z, x, kaug, qaug, dbias, cvec, lamv, subln_g[:, None], final_g[None, :], w_out.astype(BF16))


def kernel(x, ab_norm_g, ab_w_in, ab_ssd_conv_w, ab_ssd_conv_b, ab_dt_bias, ab_a_log, ab_d_skip, ab_ssd_norm_g,
           ab_ml_conv_w, ab_ml_conv_b, ab_i_bias, ab_f_bias, ab_ml_norm_g, ab_w_out,
           c_norm_g, c_w_in, c_lam_q1, c_lam_k1, c_lam_q2, c_lam_k2, c_subln_g, c_w_out, final_norm_g):
    x1 = _layer_ab(x, ab_norm_g[0], ab_w_in[0], ab_ssd_conv_w[0], ab_ssd_conv_b[0], ab_dt_bias[0], ab_a_log[0],
                   ab_d_skip[0], ab_ssd_norm_g[0], ab_ml_conv_w[0], ab_ml_conv_b[0], ab_i_bias[0], ab_f_bias[0],
                   ab_ml_norm_g[0], ab_w_out[0])
    q, k, vt, z = _layer_c_proj(x1, c_norm_g[0], c_w_in[0])
    return _layer_c_attn(x1, q, k, vt, z, c_lam_q1[0], c_lam_k1[0], c_lam_q2[0], c_lam_k2[0],
                         c_subln_g[0], c_w_out[0], final_norm_g)
```

```python
import functools
import math

import numpy as np
import jax
import jax.numpy as jnp
from jax import lax
from jax.experimental import pallas as pl
from jax.experimental.pallas import tpu as pltpu

F32 = jnp.float32
BF16 = jnp.bfloat16

D_MODEL = 1024
CHUNK = 64
EPS = 1e-6
NEG = -1e30

SSD_HEADS = 16
SSD_HEAD_DIM = 64
SSD_INNER = 1024
SSD_GROUPS = 2
SSD_HPG = 8
SSD_STATE = 128
SSD_GN = 256
SSD_XBC = 1536
SSD_CONV = 4

ML_HEADS = 4
ML_QK_DIM = 128
ML_V_DIM = 256
ML_QK = 512
ML_INNER = 1024
ML_CONV = 4

DA_HEADS = 8
DA_QK_DIM = 64
DA_V_DIM = 128
DA_QK = 1024
DA_INNER = 1024

AB_SIZES = (SSD_INNER, SSD_XBC, SSD_HEADS, ML_INNER, ML_QK, ML_QK, ML_INNER, ML_HEADS, ML_HEADS, ML_INNER)
AB_OFFS = tuple(int(s) for s in np.cumsum((0,) + AB_SIZES))

LANES = 128
SUBLANES = 8

T_AB = 256
SM_DT = 0
SM_IG = 16
SM_FG = 20
VMEM_LIMIT = 56 * 1024 * 1024


def _split3(a):
    hi = a.astype(BF16)
    r = a - hi.astype(F32)
    mid = r.astype(BF16)
    lo = (r - mid.astype(F32)).astype(BF16)
    return hi, mid, lo


def _dot(a, b):
    return jnp.dot(a, b, preferred_element_type=F32)


def _dot_nt(a, b):
    return lax.dot_general(a, b, (((1,), (1,)), ((), ())), preferred_element_type=F32)


def _dot_tn(a, b):
    return lax.dot_general(a, b, (((0,), (0,)), ((), ())), preferred_element_type=F32)


def _sigmoid(v):
    return 1.0 / (1.0 + jnp.exp(-v))


def _silu(v):
    return v * _sigmoid(v)


def _ab_kernel(x_ref, g_ref, wza_ref, wxbc_ref, wzb_ref, wqk_ref, wv_ref, wog_ref, wsm_ref,
               scw_ref, scb_ref, smb_ref, sma_ref, dskip_ref, sng_ref,
               mcw_ref, mcb_ref, mng_ref, wout_ref, tril_ref, e16_ref,
               out_ref,
               xbcp, qkp, sst, cst, nst, mst, ybuf):
    T = T_AB
    c = pl.program_id(1)

    @pl.when(c == 0)
    def _():
        xbcp[0:SUBLANES, :] = jnp.zeros((SUBLANES, SSD_XBC), F32)
        qkp[0:SUBLANES, :] = jnp.zeros((SUBLANES, 2 * ML_QK), F32)
        sst[...] = jnp.zeros_like(sst)
        cst[...] = jnp.zeros_like(cst)
        nst[...] = jnp.zeros_like(nst)
        mst[...] = jnp.zeros_like(mst)

    x = x_ref[0]
    ms = jnp.mean(x * x, axis=-1, keepdims=True)
    h = (x * lax.rsqrt(ms + EPS) * g_ref[...]).astype(BF16)

    def conv(pad_ref, w_ref, cw_ref, cb_ref):
        cur = _dot(h, w_ref[...])
        pad_ref[SUBLANES:SUBLANES + T, :] = cur
        acc = cb_ref[...] + cw_ref[3:4, :] * cur
        for k in range(3):
            off = SUBLANES - 3 + k
            acc = acc + cw_ref[k:k + 1, :] * pad_ref[off:off + T, :]
        pad_ref[0:SUBLANES, :] = pad_ref[T:T + SUBLANES, :]
        return _silu(acc)

    xbc = conv(xbcp, wxbc_ref, scw_ref, scb_ref)
    qk = conv(qkp, wqk_ref, mcw_ref, mcb_ref)

    lane = lax.broadcasted_iota(jnp.int32, (T, LANES), 1)
    sv = _dot(h, wsm_ref[...]) + smb_ref[...]
    e = jnp.log1p(jnp.exp(-jnp.abs(sv)))
    dt = jnp.where(lane < SM_IG, jnp.maximum(sv, 0.0) + e, 0.0)
    fgl = jnp.minimum(sv, 0.0) - e
    cs_in = jnp.where(lane < SM_IG, dt * sma_ref[...],
                      jnp.where(lane < SM_FG, 0.0, jnp.where(lane < SM_FG + ML_HEADS, fgl, 0.0)))
    tril = tril_ref[...]
    c_hi, c_mid, c_lo = _split3(cs_in)
    cs = _dot(tril, c_hi) + _dot(tril, c_mid) + _dot(tril, c_lo)
    scol = jnp.where(lane < SM_IG, cs, jnp.where(lane < SM_FG, sv, cs))
    srow = scol.T
    cs_end = cs[T - 1:T, :]
    eacs = jnp.where(lane < SM_IG, jnp.exp(cs), 0.0)
    toend = jnp.where(lane < SM_IG, jnp.exp(cs_end - cs), 0.0)

    e16 = e16_ref[...]

    def expand(a):
        return _dot(jnp.concatenate(_split3(a), axis=1), e16)

    dt_x = expand(dt)
    eacs_x = expand(eacs)
    toend_x = expand(toend)

    rowi = lax.broadcasted_iota(jnp.int32, (T, T), 0)
    coli = lax.broadcasted_iota(jnp.int32, (T, T), 1)
    causal = rowi >= coli
    lane64 = lane < SSD_HEAD_DIM

    xs = xbc[:, :SSD_INNER]
    xdt = xs * dt_x
    xdt_end = (xdt * toend_x).astype(BF16)
    eend_x = eacs_x[T - 1:T, :]
    za = _dot(h, wza_ref[...])
    gate_a = _silu(za)
    for g in range(SSD_GROUPS):
        bm = xbc[:, SSD_INNER + g * SSD_STATE:SSD_INNER + (g + 1) * SSD_STATE].astype(BF16)
        cm = xbc[:, SSD_INNER + SSD_GN + g * SSD_STATE:SSD_INNER + SSD_GN + (g + 1) * SSD_STATE].astype(BF16)
        cb = _dot_nt(cm, bm)
        gs = slice(g * 512, (g + 1) * 512)
        s_old = sst[g]
        yoff = _dot(cm, s_old.astype(BF16)) * eacs_x[:, gs]
        for p in range(SSD_HPG // 2):
            hd0 = g * SSD_HPG + 2 * p
            ms_ = []
            for hd in (hd0, hd0 + 1):
                seg = scol[:, hd:hd + 1] - srow[hd:hd + 1, :]
                dec = jnp.exp(jnp.where(causal, seg, NEG))
                ms_.append((cb * dec).astype(BF16))
            mcat = jnp.concatenate(ms_, axis=1)
            ls = slice(hd0 * SSD_HEAD_DIM, (hd0 + 2) * SSD_HEAD_DIM)
            slab = xdt[:, ls]
            bd = jnp.concatenate([jnp.where(lane64, slab, 0.0).astype(BF16),
                                  jnp.where(lane64, 0.0, slab).astype(BF16)], axis=0)
            y = _dot(mcat, bd) + yoff[:, 2 * p * SSD_HEAD_DIM:(2 * p + 2) * SSD_HEAD_DIM]
            y = y + xs[:, ls] * dskip_ref[:, ls]
            ybuf[:, ls] = y * gate_a[:, ls]
        sst[g] = s_old * eend_x[:, gs] + _dot_tn(bm, xdt_end[:, gs])
    for g in range(SSD_GROUPS):
        gs = slice(g * 512, (g + 1) * 512)
        yg = ybuf[:, gs]
        inv = lax.rsqrt(jnp.mean(yg * yg, axis=-1, keepdims=True) + EPS)
        ybuf[:, gs] = yg * inv * sng_ref[:, gs]

    vv = _dot(h, wv_ref[...])
    og = _dot(h, wog_ref[...])
    zb = _dot(h, wzb_ref[...])
    for hd in range(ML_HEADS):
        q_h = qk[:, hd * ML_QK_DIM:(hd + 1) * ML_QK_DIM]
        k_h = qk[:, ML_QK + hd * ML_QK_DIM:ML_QK + (hd + 1) * ML_QK_DIM] * (ML_QK_DIM ** -0.5)
        vs = slice(hd * ML_V_DIM, (hd + 1) * ML_V_DIM)
        v_h = vv[:, vs].astype(BF16)
        q_b = q_h.astype(BF16)
        b_c = scol[:, SM_FG + hd:SM_FG + hd + 1]
        b_r = srow[SM_FG + hd:SM_FG + hd + 1, :]
        ig_c = scol[:, SM_IG + hd:SM_IG + hd + 1]
        ig_r = srow[SM_IG + hd:SM_IG + hd + 1, :]
        mp = mst[hd:hd + 1, 0:1]
        c_old = cst[hd]
        n_old = nst[hd:hd + 1, :]
        dlog = jnp.where(causal, b_c - b_r + ig_r, NEG)
        inter = b_c + mp
        m_t = jnp.maximum(inter, jnp.max(dlog, axis=1, keepdims=True))
        w_intra = jnp.exp(dlog - m_t)
        w_inter = jnp.exp(inter - m_t)
        s = _dot_nt(q_b, k_h.astype(BF16)) * w_intra
        num = _dot(s.astype(BF16), v_h) + w_inter * _dot(q_b, c_old.astype(BF16))
        den = jnp.sum(s, axis=1, keepdims=True) + w_inter * jnp.sum(q_h * n_old, axis=1, keepdims=True)
        hh = num / jnp.maximum(jnp.abs(den), jnp.exp(-m_t))
        b_end = b_c[T - 1:T, :]
        glog = b_end - b_c + ig_c
        m_new = jnp.maximum(b_end + mp, jnp.max(glog, axis=0, keepdims=True))
        kg = k_h * jnp.exp(glog - m_new)
        dec = jnp.exp(b_end + mp - m_new)
        cst[hd] = dec * c_old + _dot_tn(kg.astype(BF16), v_h)
        nst[hd:hd + 1, :] = dec * n_old + jnp.sum(kg, axis=0, keepdims=True)
        mst[hd:hd + 1, :] = jnp.broadcast_to(m_new, (1, LANES))
        ho = hh * _sigmoid(og[:, vs])
        inv = lax.rsqrt(jnp.mean(ho * ho, axis=-1, keepdims=True) + EPS)
        ybuf[:, ML_INNER + hd * ML_V_DIM:ML_INNER + (hd + 1) * ML_V_DIM] = (
            ho * inv * mng_ref[:, vs] * _silu(zb[:, vs]))

    y = _dot(ybuf[...].astype(BF16), wout_ref[...])
    out_ref[0] = x + y


def _const_spec(shape):
    nd = len(shape)
    return pl.BlockSpec(shape, lambda b, c: (0,) * nd, pipeline_mode=pl.Buffered(1))


def _layer_ab(x, norm_g, w_in, ssd_cw, ssd_cb, dt_bias, a_log, d_skip, ssd_ng,
              ml_cw, ml_cb, i_bias, f_bias, ml_ng, w_out):
    B, L, D = x.shape
    T = T_AB
    o = AB_OFFS
    seg = lambda i: w_in[:, o[i]:o[i + 1]]
    wza, wxbc, wdt, wzb, wq, wk, wv, wig, wfg, wog = [seg(i) for i in range(10)]
    wsm = jnp.concatenate([wdt, wig, wfg, jnp.zeros((D, LANES - SM_FG - ML_HEADS), F32)], axis=1)
    wqk = jnp.concatenate([wq, wk], axis=1)
    pad = jnp.zeros((LANES - SM_FG - ML_HEADS,), F32)
    smb = jnp.concatenate([dt_bias, i_bias, f_bias, pad])[None, :]
    sma = jnp.concatenate([-jnp.exp(a_log), jnp.zeros((LANES - SSD_HEADS,), F32)])[None, :]
    dskip = jnp.repeat(d_skip, SSD_HEAD_DIM)[None, :]
    tril = jnp.asarray(np.tril(np.ones((T, T), np.float32)), BF16)
    e16_np = np.zeros((LANES, SSD_INNER), np.float32)
    for hd in range(SSD_HEADS):
        e16_np[hd, hd * SSD_HEAD_DIM:(hd + 1) * SSD_HEAD_DIM] = 1.0
    e16 = jnp.asarray(np.concatenate([e16_np] * 3, axis=0), BF16)
    bf = lambda a: a.astype(BF16)
    row = lambda a: a[None, :]
    args = (x, row(norm_g), bf(wza), bf(wxbc), bf(wzb), bf(wqk), bf(wv), bf(wog), bf(wsm),
            ssd_cw, row(ssd_cb), smb, sma, dskip, row(ssd_ng),
            ml_cw, row(ml_cb), row(ml_ng), bf(w_out), tril, e16)
    in_specs = [pl.BlockSpec((1, T, D), lambda b, c: (b, c, 0))]
    in_specs += [_const_spec(a.shape) for a in args[1:]]
    return pl.pallas_call(
        _ab_kernel,
        grid=(B, L // T),
        in_specs=in_specs,
        out_specs=pl.BlockSpec((1, T, D), lambda b, c: (b, c, 0)),
        out_shape=jax.ShapeDtypeStruct((B, L, D), F32),
        scratch_shapes=[
            pltpu.VMEM((T + SUBLANES, SSD_XBC), F32),
            pltpu.VMEM((T + SUBLANES, 2 * ML_QK), F32),
            pltpu.VMEM((SSD_GROUPS, SSD_STATE, SSD_HPG * SSD_HEAD_DIM), F32),
            pltpu.VMEM((ML_HEADS, ML_QK_DIM, ML_V_DIM), F32),
            pltpu.VMEM((SUBLANES, ML_QK_DIM), F32),
            pltpu.VMEM((SUBLANES, LANES), F32),
            pltpu.VMEM((T, SSD_INNER + ML_INNER), F32),
        ],
        compiler_params=pltpu.CompilerParams(
            dimension_semantics=("arbitrary", "arbitrary"), vmem_limit_bytes=VMEM_LIMIT),
        name="layer_ab",
    )(*args)


T_Q = 256
T_S = 256
T_KV = 2048
V_ROWS = DA_V_DIM + 16
LAYER_C = 1
LAMBDA_INIT = 0.8 - 0.6 * math.exp(-0.3 * LAYER_C)
ALIBI_SLOPES = tuple(2.0 ** (-8.0 * (h + 1) / DA_HEADS) for h in range(DA_HEADS))
LOG2E = 1.4426950408889634


def _cproj_kernel(x_ref, g_ref, wq_ref, wk_ref, wv_ref, wz_ref, q_ref, k_ref, vt_ref, z_ref):
    x = x_ref[0]
    ms = jnp.mean(x * x, axis=-1, keepdims=True)
    h = (x * lax.rsqrt(ms + EPS) * g_ref[...]).astype(BF16)
    q_ref[0] = _dot(h, wq_ref[...]).astype(BF16)
    z_ref[0] = _dot(h, wz_ref[...]).astype(BF16)
    kk = _dot(h, wk_ref[...])
    vv = _dot(h, wv_ref[...])
    tm = x.shape[0]
    for hd in range(DA_HEADS):
        ls = slice(hd * DA_V_DIM, (hd + 1) * DA_V_DIM)
        k_ref[0, hd] = kk[:, ls].astype(BF16)
        vt_ref[0, hd, 0, 0:DA_V_DIM, :] = vv[:, ls].T.astype(BF16)
        vt_ref[0, hd, 0, DA_V_DIM:V_ROWS, :] = jnp.ones((V_ROWS - DA_V_DIM, tm), BF16)


def _layer_c_proj(x, norm_g, w_in):
    B, L, D = x.shape
    tm = T_S
    wq, wk, wv, wz = [w_in[:, i * DA_QK:(i + 1) * DA_QK].astype(BF16) for i in range(4)]
    cspec = lambda shape: pl.BlockSpec(shape, lambda b, i: (0,) * len(shape), pipeline_mode=pl.Buffered(1))
    return pl.pallas_call(
        _cproj_kernel,
        grid=(B, L // tm),
        in_specs=[pl.BlockSpec((1, tm, D), lambda b, i: (b, i, 0)), cspec((1, D)),
                  cspec((D, DA_QK)), cspec((D, DA_QK)), cspec((D, DA_INNER)), cspec((D, DA_INNER))],
        out_specs=[pl.BlockSpec((1, tm, DA_QK), lambda b, i: (b, i, 0)),
                   pl.BlockSpec((1, DA_HEADS, tm, 2 * DA_QK_DIM), lambda b, i: (b, 0, i, 0)),
                   pl.BlockSpec((1, DA_HEADS, 1, V_ROWS, tm), lambda b, i: (b, 0, i, 0, 0)),
                   pl.BlockSpec((1, tm, DA_INNER), lambda b, i: (b, i, 0))],
        out_shape=[jax.ShapeDtypeStruct((B, L, DA_QK), BF16),
                   jax.ShapeDtypeStruct((B, DA_HEADS, L, 2 * DA_QK_DIM), BF16),
                   jax.ShapeDtypeStruct((B, DA_HEADS, L // tm, V_ROWS, tm), BF16),
                   jax.ShapeDtypeStruct((B, L, DA_INNER), BF16)],
        compiler_params=pltpu.CompilerParams(
            dimension_semantics=("arbitrary", "arbitrary"), vmem_limit_bytes=VMEM_LIMIT),
        name="layer_c_proj",
    )(x, norm_g[None, :], wq, wk, wv, wz)


def _attn_kernel(qi_ref, kj_ref, q_ref, k_ref, vt_ref, z_ref, x_ref, kaug_ref, qaug_ref, dbias_ref, cvec_ref, lamv_ref,
                 subg_ref, fng_ref, wout_ref, out_ref, qt_scr, m_scr, acc_scr, st_scr, cmax_scr, pt_scr, alpha_scr,
                 ybuf):
    t = pl.program_id(1)
    i = qi_ref[t]
    j = kj_ref[t]
    q0 = i * T_Q
    k_base = j * T_KV
    qk_rows = 2 * DA_QK_DIM
    is_last_kv = q0 - k_base < T_KV

    @pl.when(j == 0)
    def _():
        m_scr[...] = jnp.full(m_scr.shape, NEG, F32)
        acc_scr[...] = jnp.zeros_like(acc_scr)
        rowi = lax.broadcasted_iota(jnp.int32, (qk_rows, T_Q), 0)
        for hd in range(DA_HEADS):
            qh = q_ref[0, :, hd * qk_rows:(hd + 1) * qk_rows].astype(F32) * (DA_QK_DIM ** -0.5 * LOG2E)
            qht = qh.T
            qt_scr[2 * hd, 0:qk_rows, :] = jnp.where(rowi < DA_QK_DIM, qht, 0.0).astype(BF16)
            qt_scr[2 * hd + 1, 0:qk_rows, :] = jnp.where(rowi < DA_QK_DIM, 0.0, qht).astype(BF16)
            qt_scr[2 * hd, qk_rows:, :] = qaug_ref[hd]
            qt_scr[2 * hd + 1, qk_rows:, :] = qaug_ref[hd]

    n_hh = 2 * DA_HEADS

    def stage_a(hh, scores):
        st = scores(hh)
        st_scr[hh] = st
        cmax_scr[hh] = jnp.max(st, axis=0, keepdims=True)

    def stage_b(hh, coff):
        m_old = m_scr[hh]
        m_new = jnp.maximum(m_old, cmax_scr[hh] + coff)
        alpha_scr[hh] = jnp.exp2(m_old - m_new)
        pt_scr[hh] = jnp.exp2(st_scr[hh] - (m_new - coff)).astype(BF16)
        m_scr[hh] = m_new

    def stage_c(hh, s):
        acc_scr[hh] = acc_scr[hh] * alpha_scr[hh] + _dot(vt_ref[0, hh // 2, s], pt_scr[hh])

    def drain(coff, s):
        for hh in range(n_hh):
            stage_b(hh, coff(hh))
            if hh > 0:
                stage_c(hh - 1, s)
        stage_c(n_hh - 1, s)

    def process_kv_block():
        n_full = jnp.minimum(T_KV // T_S, (q0 - k_base) // T_S)
        kaug = kaug_ref[...]

        def full_scores(s):
            rows = pl.ds(pl.multiple_of(s * T_S, T_S), T_S)

            def scores(hh):
                lhs = jnp.concatenate([k_ref[0, hh // 2, rows, :], kaug], axis=1)
                return _dot(lhs, qt_scr[hh])
            return scores

        def full_coff(s):
            dk = jnp.full((1, T_Q), k_base + s * T_S - q0, jnp.int32).astype(F32)
            return lambda hh: cvec_ref[hh // 2] * dk

        @pl.when(n_full > 0)
        def _():
            sc0 = full_scores(0)
            for hh in range(n_hh):
                stage_a(hh, sc0)

            def sub_body(s, carry):
                sc, coff = full_scores(s + 1), full_coff(s)
                for hh in range(n_hh):
                    stage_b(hh, coff(hh))
                    stage_a(hh, sc)
                for hh in range(n_hh):
                    stage_c(hh, s)
                return carry

            lax.fori_loop(0, n_full - 1, sub_body, 0)
            s_last = n_full - 1
            coff = full_coff(s_last)
            drain(lambda hh: coff(hh), s_last)

        @pl.when(is_last_kv)
        def _():
            sd = (q0 - k_base) // T_S
            rows = pl.ds(pl.multiple_of(sd * T_S, T_S), T_S)

            def scores(hh):
                return _dot(k_ref[0, hh // 2, rows, :], qt_scr[hh, 0:qk_rows, :]) + dbias_ref[hh // 2]

            for hh in range(n_hh):
                stage_a(hh, scores)
            drain(lambda hh: 0.0, sd)

    process_kv_block()

    @pl.when(is_last_kv)
    def _():
        lv = lamv_ref[...]
        lam = (jnp.exp(jnp.sum(lv[0:1] * lv[1:2], axis=1, keepdims=True))
               - jnp.exp(jnp.sum(lv[2:3] * lv[3:4], axis=1, keepdims=True)) + LAMBDA_INIT)
        for hd in range(DA_HEADS):
            a1 = acc_scr[2 * hd]
            a2 = acc_scr[2 * hd + 1]
            o1 = a1[0:DA_V_DIM] / a1[DA_V_DIM:DA_V_DIM + 1]
            o2 = a2[0:DA_V_DIM] / a2[DA_V_DIM:DA_V_DIM + 1]
            ot = o1 - lam * o2
            inv = lax.rsqrt(jnp.mean(ot * ot, axis=0, keepdims=True) + EPS)
            ot = ot * inv * subg_ref[...] * (1.0 - LAMBDA_INIT)
            ls = slice(hd * DA_V_DIM, (hd + 1) * DA_V_DIM)
            ybuf[:, ls] = ot.T * _silu(z_ref[0, :, ls].astype(F32))
        y = _dot(ybuf[...].astype(BF16), wout_ref[...])
        xo = x_ref[0] + y
        inv = lax.rsqrt(jnp.mean(xo * xo, axis=-1, keepdims=True) + EPS)
        out_ref[0] = xo * inv * fng_ref[...]


def _np_split3(c):
    import ml_dtypes
    rb = lambda a: a.astype(ml_dtypes.bfloat16).astype(np.float32)
    hi = rb(c)
    mid = rb(c - hi)
    lo = rb(c - hi - mid)
    return hi, mid, lo


def _attn_tables():
    c32 = (np.asarray(ALIBI_SLOPES, np.float64) * LOG2E).astype(np.float32)
    jj = np.arange(T_S, dtype=np.float32)[:, None]
    ii = np.arange(T_Q, dtype=np.float32)[None, :]
    allowed = (jj // CHUNK) <= (ii // CHUNK)
    kaug = np.zeros((T_S, LANES), np.float32)
    kaug[:, 0:3] = jj
    qaug = np.zeros((DA_HEADS, LANES, T_Q), np.float32)
    for r, part in enumerate(_np_split3(c32)):
        qaug[:, r, :] = part[:, None]
    dbias = np.where(allowed[None], c32[:, None, None] * (ii - np.abs(ii - jj))[None], np.float32(NEG))
    cvec = np.broadcast_to(c32[:, None, None], (DA_HEADS, 1, T_Q))
    return (jnp.asarray(kaug, BF16), jnp.asarray(qaug, BF16), jnp.asarray(dbias, F32), jnp.asarray(cvec, F32))


def _layer_c_attn(x, q, k, vt, z, lam_q1, lam_k1, lam_q2, lam_k2, subln_g, w_out, final_g):
    B, L, D = x.shape
    lamv = jnp.zeros((SUBLANES, LANES), F32)
    for r, v in enumerate((lam_q1, lam_k1, lam_q2, lam_k2)):
        lamv = lamv.at[r, :DA_QK_DIM].set(v)
    kaug, qaug, dbias, cvec = _attn_tables()
    pairs = [(i, j) for i in range(L // T_Q) for j in range((i * T_Q) // T_KV + 1)]
    qi = jnp.asarray([p[0] for p in pairs], jnp.int32)
    kj = jnp.asarray([p[1] for p in pairs], jnp.int32)
    cspec = lambda shape: pl.BlockSpec(shape, lambda b, t, qi, kj: (0,) * len(shape), pipeline_mode=pl.Buffered(1))
    row_spec = pl.BlockSpec((1, T_Q, D), lambda b, t, qi, kj: (b, qi[t], 0))
    in_specs = [row_spec,
                pl.BlockSpec((1, DA_HEADS, T_KV, 2 * DA_QK_DIM), lambda b, t, qi, kj: (b, 0, kj[t], 0)),
                pl.BlockSpec((1, DA_HEADS, T_KV // T_S, V_ROWS, T_S), lambda b, t, qi, kj: (b, 0, kj[t], 0, 0)),
                row_spec, row_spec,
                cspec((T_S, LANES)), cspec((DA_HEADS, LANES, T_Q)), cspec((DA_HEADS, T_S, T_Q)),
                cspec((DA_HEADS, 1, T_Q)), cspec((SUBLANES, LANES)), cspec((DA_V_DIM, 1)),
                cspec((1, D)), cspec((DA_INNER, D))]
    return pl.pallas_call(
        _attn_kernel,
        grid_spec=pltpu.PrefetchScalarGridSpec(
            num_scalar_prefetch=2,
            grid=(B, len(pairs)),
            in_specs=in_specs,
            out_specs=row_spec,
            scratch_shapes=_attn_scratch()),
        out_shape=jax.ShapeDtypeStruct((B, L, D), F32),
        compiler_params=pltpu.CompilerParams(
            dimension_semantics=("arbitrary", "arbitrary"), vmem_limit_bytes=VMEM_LIMIT),
        name="layer_c_attn",
    )(qi, kj, q, k, vt, z, x, kaug, qaug, dbias, cvec, lamv, subln_g[:, None], final_g[None, :], w_out.astype(BF16))


def _attn_scratch():
    return [
            pltpu.VMEM((2 * DA_HEADS, 2 * DA_QK_DIM + LANES, T_Q), BF16),
            pltpu.VMEM((2 * DA_HEADS, 1, T_Q), F32),
            pltpu.VMEM((2 * DA_HEADS, V_ROWS, T_Q), F32),
            pltpu.VMEM((2 * DA_HEADS, T_S, T_Q), F32),
            pltpu.VMEM((2 * DA_HEADS, 1, T_Q), F32),
            pltpu.VMEM((2 * DA_HEADS, T_S, T_Q), BF16),
            pltpu.VMEM((2 * DA_HEADS, 1, T_Q), F32),
            pltpu.VMEM((T_Q, DA_INNER), F32),
    ]


def kernel(x, ab_norm_g, ab_w_in, ab_ssd_conv_w, ab_ssd_conv_b, ab_dt_bias, ab_a_log, ab_d_skip, ab_ssd_norm_g,
           ab_ml_conv_w, ab_ml_conv_b, ab_i_bias, ab_f_bias, ab_ml_norm_g, ab_w_out,
           c_norm_g, c_w_in, c_lam_q1, c_lam_k1, c_lam_q2, c_lam_k2, c_subln_g, c_w_out, final_norm_g):
    x1 = _layer_ab(x, ab_norm_g[0], ab_w_in[0], ab_ssd_conv_w[0], ab_ssd_conv_b[0], ab_dt_bias[0], ab_a_log[0],
                   ab_d_skip[0], ab_ssd_norm_g[0], ab_ml_conv_w[0], ab_ml_conv_b[0], ab_i_bias[0], ab_f_bias[0],
                   ab_ml_norm_g[0], ab_w_out[0])
    q, k, vt, z = _layer_c_proj(x1, c_norm_g[0], c_w_in[0])
    return _layer_c_attn(x1, q, k, vt, z, c_lam_q1[0], c_lam_k1[0], c_lam_q2[0], c_lam_k2[0],
                         c_subln_g[0], c_w_out[0], final_norm_g)
```

```python
import functools
import math

import numpy as np
import jax
import jax.numpy as jnp
from jax import lax
from jax.experimental import pallas as pl
from jax.experimental.pallas import tpu as pltpu

F32 = jnp.float32
BF16 = jnp.bfloat16

D_MODEL = 1024
CHUNK = 64
EPS = 1e-6
NEG = -1e30

SSD_HEADS = 16
SSD_HEAD_DIM = 64
SSD_INNER = 1024
SSD_GROUPS = 2
SSD_HPG = 8
SSD_STATE = 128
SSD_GN = 256
SSD_XBC = 1536
SSD_CONV = 4

ML_HEADS = 4
ML_QK_DIM = 128
ML_V_DIM = 256
ML_QK = 512
ML_INNER = 1024
ML_CONV = 4

DA_HEADS = 8
DA_QK_DIM = 64
DA_V_DIM = 128
DA_QK = 1024
DA_INNER = 1024

AB_SIZES = (SSD_INNER, SSD_XBC, SSD_HEADS, ML_INNER, ML_QK, ML_QK, ML_INNER, ML_HEADS, ML_HEADS, ML_INNER)
AB_OFFS = tuple(int(s) for s in np.cumsum((0,) + AB_SIZES))

LANES = 128
SUBLANES = 8

T_AB = 256
SM_DT = 0
SM_IG = 16
SM_FG = 20
VMEM_LIMIT = 56 * 1024 * 1024


def _split3(a):
    hi = a.astype(BF16)
    r = a - hi.astype(F32)
    mid = r.astype(BF16)
    lo = (r - mid.astype(F32)).astype(BF16)
    return hi, mid, lo


def _dot(a, b):
    return jnp.dot(a, b, preferred_element_type=F32)


def _dot_nt(a, b):
    return lax.dot_general(a, b, (((1,), (1,)), ((), ())), preferred_element_type=F32)


def _dot_tn(a, b):
    return lax.dot_general(a, b, (((0,), (0,)), ((), ())), preferred_element_type=F32)


def _sigmoid(v):
    return 1.0 / (1.0 + jnp.exp(-v))


def _silu(v):
    return v * _sigmoid(v)


def _ab_kernel(x_ref, g_ref, wza_ref, wxbc_ref, wzb_ref, wqk_ref, wv_ref, wog_ref, wsm_ref,
               scw_ref, scb_ref, smb_ref, sma_ref, dskip_ref, sng_ref,
               mcw_ref, mcb_ref, mng_ref, wout_ref, tril_ref, e16_ref,
               out_ref,
               xbcp, qkp, sst, cst, nst, mst, ybuf):
    T = T_AB
    c = pl.program_id(1)

    @pl.when(c == 0)
    def _():
        xbcp[0:SUBLANES, :] = jnp.zeros((SUBLANES, SSD_XBC), F32)
        qkp[0:SUBLANES, :] = jnp.zeros((SUBLANES, 2 * ML_QK), F32)
        sst[...] = jnp.zeros_like(sst)
        cst[...] = jnp.zeros_like(cst)
        nst[...] = jnp.zeros_like(nst)
        mst[...] = jnp.zeros_like(mst)

    x = x_ref[0]
    ms = jnp.mean(x * x, axis=-1, keepdims=True)
    h = (x * lax.rsqrt(ms + EPS) * g_ref[...]).astype(BF16)

    def conv(pad_ref, w_ref, cw_ref, cb_ref):
        cur = _dot(h, w_ref[...])
        pad_ref[SUBLANES:SUBLANES + T, :] = cur
        acc = cb_ref[...] + cw_ref[3:4, :] * cur
        for k in range(3):
            off = SUBLANES - 3 + k
            acc = acc + cw_ref[k:k + 1, :] * pad_ref[off:off + T, :]
        pad_ref[0:SUBLANES, :] = pad_ref[T:T + SUBLANES, :]
        return _silu(acc)

    xbc = conv(xbcp, wxbc_ref, scw_ref, scb_ref)
    qk = conv(qkp, wqk_ref, mcw_ref, mcb_ref)

    lane = lax.broadcasted_iota(jnp.int32, (T, LANES), 1)
    sv = _dot(h, wsm_ref[...]) + smb_ref[...]
    e = jnp.log1p(jnp.exp(-jnp.abs(sv)))
    dt = jnp.where(lane < SM_IG, jnp.maximum(sv, 0.0) + e, 0.0)
    fgl = jnp.minimum(sv, 0.0) - e
    cs_in = jnp.where(lane < SM_IG, dt * sma_ref[...],
                      jnp.where(lane < SM_FG, 0.0, jnp.where(lane < SM_FG + ML_HEADS, fgl, 0.0)))
    tril = tril_ref[...]
    c_hi, c_mid, c_lo = _split3(cs_in)
    cs = _dot(tril, c_hi) + _dot(tril, c_mid) + _dot(tril, c_lo)
    scol = jnp.where(lane < SM_IG, cs, jnp.where(lane < SM_FG, sv, cs))
    srow = scol.T
    cs_end = cs[T - 1:T, :]
    eacs = jnp.where(lane < SM_IG, jnp.exp(cs), 0.0)
    toend = jnp.where(lane < SM_IG, jnp.exp(cs_end - cs), 0.0)

    e16 = e16_ref[...]

    def expand(a):
        return _dot(jnp.concatenate(_split3(a), axis=1), e16)

    dt_x = expand(dt)
    eacs_x = expand(eacs)
    toend_x = expand(toend)

    rowi = lax.broadcasted_iota(jnp.int32, (T, T), 0)
    coli = lax.broadcasted_iota(jnp.int32, (T, T), 1)
    causal = rowi >= coli
    lane64 = lane < SSD_HEAD_DIM

    xs = xbc[:, :SSD_INNER]
    xdt = xs * dt_x
    xdt_end = (xdt * toend_x).astype(BF16)
    eend_x = eacs_x[T - 1:T, :]
    za = _dot(h, wza_ref[...])
    gate_a = _silu(za)
    for g in range(SSD_GROUPS):
        bm = xbc[:, SSD_INNER + g * SSD_STATE:SSD_INNER + (g + 1) * SSD_STATE].astype(BF16)
        cm = xbc[:, SSD_INNER + SSD_GN + g * SSD_STATE:SSD_INNER + SSD_GN + (g + 1) * SSD_STATE].astype(BF16)
        cb = _dot_nt(cm, bm)
        gs = slice(g * 512, (g + 1) * 512)
        s_old = sst[g]
        yoff = _dot(cm, s_old.astype(BF16)) * eacs_x[:, gs]
        for p in range(SSD_HPG // 2):
            hd0 = g * SSD_HPG + 2 * p
            ms_ = []
            for hd in (hd0, hd0 + 1):
                seg = scol[:, hd:hd + 1] - srow[hd:hd + 1, :]
                dec = jnp.exp(jnp.where(causal, seg, NEG))
                ms_.append((cb * dec).astype(BF16))
            mcat = jnp.concatenate(ms_, axis=1)
            ls = slice(hd0 * SSD_HEAD_DIM, (hd0 + 2) * SSD_HEAD_DIM)
            slab = xdt[:, ls]
            bd = jnp.concatenate([jnp.where(lane64, slab, 0.0).astype(BF16),
                                  jnp.where(lane64, 0.0, slab).astype(BF16)], axis=0)
            y = _dot(mcat, bd) + yoff[:, 2 * p * SSD_HEAD_DIM:(2 * p + 2) * SSD_HEAD_DIM]
            y = y + xs[:, ls] * dskip_ref[:, ls]
            ybuf[:, ls] = y * gate_a[:, ls]
        sst[g] = s_old * eend_x[:, gs] + _dot_tn(bm, xdt_end[:, gs])
    for g in range(SSD_GROUPS):
        gs = slice(g * 512, (g + 1) * 512)
        yg = ybuf[:, gs]
        inv = lax.rsqrt(jnp.mean(yg * yg, axis=-1, keepdims=True) + EPS)
        ybuf[:, gs] = yg * inv * sng_ref[:, gs]

    vv = _dot(h, wv_ref[...])
    og = _dot(h, wog_ref[...])
    zb = _dot(h, wzb_ref[...])
    for hd in range(ML_HEADS):
        q_h = qk[:, hd * ML_QK_DIM:(hd + 1) * ML_QK_DIM]
        k_h = qk[:, ML_QK + hd * ML_QK_DIM:ML_QK + (hd + 1) * ML_QK_DIM] * (ML_QK_DIM ** -0.5)
        vs = slice(hd * ML_V_DIM, (hd + 1) * ML_V_DIM)
        v_h = vv[:, vs].astype(BF16)
        q_b = q_h.astype(BF16)
        b_c = scol[:, SM_FG + hd:SM_FG + hd + 1]
        b_r = srow[SM_FG + hd:SM_FG + hd + 1, :]
        ig_c = scol[:, SM_IG + hd:SM_IG + hd + 1]
        ig_r = srow[SM_IG + hd:SM_IG + hd + 1, :]
        mp = mst[hd:hd + 1, 0:1]
        c_old = cst[hd]
        n_old = nst[hd:hd + 1, :]
        dlog = jnp.where(causal, b_c - b_r + ig_r, NEG)
        inter = b_c + mp
        m_t = jnp.maximum(inter, jnp.max(dlog, axis=1, keepdims=True))
        w_intra = jnp.exp(dlog - m_t)
        w_inter = jnp.exp(inter - m_t)
        s = _dot_nt(q_b, k_h.astype(BF16)) * w_intra
        num = _dot(s.astype(BF16), v_h) + w_inter * _dot(q_b, c_old.astype(BF16))
        den = jnp.sum(s, axis=1, keepdims=True) + w_inter * jnp.sum(q_h * n_old, axis=1, keepdims=True)
        hh = num / jnp.maximum(jnp.abs(den), jnp.exp(-m_t))
        b_end = b_c[T - 1:T, :]
        glog = b_end - b_c + ig_c
        m_new = jnp.maximum(b_end + mp, jnp.max(glog, axis=0, keepdims=True))
        kg = k_h * jnp.exp(glog - m_new)
        dec = jnp.exp(b_end + mp - m_new)
        cst[hd] = dec * c_old + _dot_tn(kg.astype(BF16), v_h)
        nst[hd:hd + 1, :] = dec * n_old + jnp.sum(kg, axis=0, keepdims=True)
        mst[hd:hd + 1, :] = jnp.broadcast_to(m_new, (1, LANES))
        ho = hh * _sigmoid(og[:, vs])
        inv = lax.rsqrt(jnp.mean(ho * ho, axis=-1, keepdims=True) + EPS)
        ybuf[:, ML_INNER + hd * ML_V_DIM:ML_INNER + (hd + 1) * ML_V_DIM] = (
            ho * inv * mng_ref[:, vs] * _silu(zb[:, vs]))

    y = _dot(ybuf[...].astype(BF16), wout_ref[...])
    out_ref[0] = x + y


def _const_spec(shape):
    nd = len(shape)
    return pl.BlockSpec(shape, lambda b, c: (0,) * nd, pipeline_mode=pl.Buffered(1))


def _layer_ab(x, norm_g, w_in, ssd_cw, ssd_cb, dt_bias, a_log, d_skip, ssd_ng,
              ml_cw, ml_cb, i_bias, f_bias, ml_ng, w_out):
    B, L, D = x.shape
    T = T_AB
    o = AB_OFFS
    seg = lambda i: w_in[:, o[i]:o[i + 1]]
    wza, wxbc, wdt, wzb, wq, wk, wv, wig, wfg, wog = [seg(i) for i in range(10)]
    wsm = jnp.concatenate([wdt, wig, wfg, jnp.zeros((D, LANES - SM_FG - ML_HEADS), F32)], axis=1)
    wqk = jnp.concatenate([wq, wk], axis=1)
    pad = jnp.zeros((LANES - SM_FG - ML_HEADS,), F32)
    smb = jnp.concatenate([dt_bias, i_bias, f_bias, pad])[None, :]
    sma = jnp.concatenate([-jnp.exp(a_log), jnp.zeros((LANES - SSD_HEADS,), F32)])[None, :]
    dskip = jnp.repeat(d_skip, SSD_HEAD_DIM)[None, :]
    tril = jnp.asarray(np.tril(np.ones((T, T), np.float32)), BF16)
    e16_np = np.zeros((LANES, SSD_INNER), np.float32)
    for hd in range(SSD_HEADS):
        e16_np[hd, hd * SSD_HEAD_DIM:(hd + 1) * SSD_HEAD_DIM] = 1.0
    e16 = jnp.asarray(np.concatenate([e16_np] * 3, axis=0), BF16)
    bf = lambda a: a.astype(BF16)
    row = lambda a: a[None, :]
    args = (x, row(norm_g), bf(wza), bf(wxbc), bf(wzb), bf(wqk), bf(wv), bf(wog), bf(wsm),
            ssd_cw, row(ssd_cb), smb, sma, dskip, row(ssd_ng),
            ml_cw, row(ml_cb), row(ml_ng), bf(w_out), tril, e16)
    in_specs = [pl.BlockSpec((1, T, D), lambda b, c: (b, c, 0))]
    in_specs += [_const_spec(a.shape) for a in args[1:]]
    return pl.pallas_call(
        _ab_kernel,
        grid=(B, L // T),
        in_specs=in_specs,
        out_specs=pl.BlockSpec((1, T, D), lambda b, c: (b, c, 0)),
        out_shape=jax.ShapeDtypeStruct((B, L, D), F32),
        scratch_shapes=[
            pltpu.VMEM((T + SUBLANES, SSD_XBC), F32),
            pltpu.VMEM((T + SUBLANES, 2 * ML_QK), F32),
            pltpu.VMEM((SSD_GROUPS, SSD_STATE, SSD_HPG * SSD_HEAD_DIM), F32),
            pltpu.VMEM((ML_HEADS, ML_QK_DIM, ML_V_DIM), F32),
            pltpu.VMEM((SUBLANES, ML_QK_DIM), F32),
            pltpu.VMEM((SUBLANES, LANES), F32),
            pltpu.VMEM((T, SSD_INNER + ML_INNER), F32),
        ],
        compiler_params=pltpu.CompilerParams(
            dimension_semantics=("arbitrary", "arbitrary"), vmem_limit_bytes=VMEM_LIMIT),
        name="layer_ab",
    )(*args)


T_Q = 256
T_S = 256
N_BUF = 8
N_AHEAD = 3
UNROLL = 2
V_ROWS = DA_V_DIM + 16
LAYER_C = 1
LAMBDA_INIT = 0.8 - 0.6 * math.exp(-0.3 * LAYER_C)
ALIBI_SLOPES = tuple(2.0 ** (-8.0 * (h + 1) / DA_HEADS) for h in range(DA_HEADS))
LOG2E = 1.4426950408889634


def _cproj_kernel(x_ref, g_ref, wq_ref, wk_ref, wv_ref, wz_ref, q_ref, k_ref, vt_ref, z_ref):
    x = x_ref[0]
    ms = jnp.mean(x * x, axis=-1, keepdims=True)
    h = (x * lax.rsqrt(ms + EPS) * g_ref[...]).astype(BF16)
    q_ref[0] = _dot(h, wq_ref[...]).astype(BF16)
    z_ref[0] = _dot(h, wz_ref[...]).astype(BF16)
    kk = _dot(h, wk_ref[...])
    vv = _dot(h, wv_ref[...])
    tm = x.shape[0]
    for hd in range(DA_HEADS):
        ls = slice(hd * DA_V_DIM, (hd + 1) * DA_V_DIM)
        k_ref[0, hd] = kk[:, ls].astype(BF16)
        vt_ref[0, hd, 0, 0:DA_V_DIM, :] = vv[:, ls].T.astype(BF16)
        vt_ref[0, hd, 0, DA_V_DIM:V_ROWS, :] = jnp.ones((V_ROWS - DA_V_DIM, tm), BF16)


def _layer_c_proj(x, norm_g, w_in):
    B, L, D = x.shape
    tm = T_S
    wq, wk, wv, wz = [w_in[:, i * DA_QK:(i + 1) * DA_QK].astype(BF16) for i in range(4)]
    cspec = lambda shape: pl.BlockSpec(shape, lambda b, i: (0,) * len(shape), pipeline_mode=pl.Buffered(1))
    return pl.pallas_call(
        _cproj_kernel,
        grid=(B, L // tm),
        in_specs=[pl.BlockSpec((1, tm, D), lambda b, i: (b, i, 0)), cspec((1, D)),
                  cspec((D, DA_QK)), cspec((D, DA_QK)), cspec((D, DA_INNER)), cspec((D, DA_INNER))],
        out_specs=[pl.BlockSpec((1, tm, DA_QK), lambda b, i: (b, i, 0)),
                   pl.BlockSpec((1, DA_HEADS, tm, 2 * DA_QK_DIM), lambda b, i: (b, 0, i, 0)),
                   pl.BlockSpec((1, DA_HEADS, 1, V_ROWS, tm), lambda b, i: (b, 0, i, 0, 0)),
                   pl.BlockSpec((1, tm, DA_INNER), lambda b, i: (b, i, 0))],
        out_shape=[jax.ShapeDtypeStruct((B, L, DA_QK), BF16),
                   jax.ShapeDtypeStruct((B, DA_HEADS, L, 2 * DA_QK_DIM), BF16),
                   jax.ShapeDtypeStruct((B, DA_HEADS, L // tm, V_ROWS, tm), BF16),
                   jax.ShapeDtypeStruct((B, L, DA_INNER), BF16)],
        compiler_params=pltpu.CompilerParams(
            dimension_semantics=("arbitrary", "arbitrary"), vmem_limit_bytes=VMEM_LIMIT),
        name="layer_c_proj",
    )(x, norm_g[None, :], wq, wk, wv, wz)


def _attn_kernel(q_ref, k_hbm, vt_hbm, z_ref, x_ref, kaug_ref, qaug_ref, dbias_ref, cvec_ref, lamv_ref,
                 subg_ref, fng_ref, wout_ref, out_ref, kbuf, vbuf, sem, qt_scr, m_scr, acc_scr, st_scr, cmax_scr,
                 pt_scr, alpha_scr, ybuf):
    b = pl.program_id(0)
    i = pl.program_id(1)
    qk_rows = 2 * DA_QK_DIM

    def kv_copies(u):
        slot = u % N_BUF
        rows = pl.ds(pl.multiple_of(u * T_S, T_S), T_S)
        return (pltpu.make_async_copy(k_hbm.at[b, :, rows, :], kbuf.at[slot], sem.at[0, slot]),
                pltpu.make_async_copy(vt_hbm.at[b, :, u], vbuf.at[slot], sem.at[1, slot]))

    def fetch(u):
        for cp in kv_copies(u):
            cp.start()

    def wait(u):
        for cp in kv_copies(u):
            cp.wait()

    for u in range(N_AHEAD):
        @pl.when(u <= i)
        def _():
            fetch(u)

    def init():
        m_scr[...] = jnp.full(m_scr.shape, NEG, F32)
        acc_scr[...] = jnp.zeros_like(acc_scr)
        pt_scr[...] = jnp.zeros_like(pt_scr)
        alpha_scr[...] = jnp.ones_like(alpha_scr)
        rowi = lax.broadcasted_iota(jnp.int32, (qk_rows, T_Q), 0)
        for hd in range(DA_HEADS):
            qh = q_ref[0, :, hd * qk_rows:(hd + 1) * qk_rows].astype(F32) * (DA_QK_DIM ** -0.5 * LOG2E)
            qht = qh.T
            qt_scr[2 * hd, 0:qk_rows, :] = jnp.where(rowi < DA_QK_DIM, qht, 0.0).astype(BF16)
            qt_scr[2 * hd + 1, 0:qk_rows, :] = jnp.where(rowi < DA_QK_DIM, 0.0, qht).astype(BF16)
            qt_scr[2 * hd, qk_rows:, :] = qaug_ref[hd]
            qt_scr[2 * hd + 1, qk_rows:, :] = qaug_ref[hd]

    init()

    n_hh = 2 * DA_HEADS
    kaug = kaug_ref[...]

    def stage_a_full(hh, u):
        lhs = jnp.concatenate([kbuf[u % N_BUF, hh // 2], kaug], axis=1)
        st = _dot(lhs, qt_scr[hh])
        st_scr[hh] = st
        cmax_scr[hh] = jnp.max(st, axis=0, keepdims=True)

    def stage_a_diag(hh, u):
        st = _dot(kbuf[u % N_BUF, hh // 2], qt_scr[hh, 0:qk_rows, :]) + dbias_ref[hh // 2]
        st_scr[hh] = st
        cmax_scr[hh] = jnp.max(st, axis=0, keepdims=True)

    def stage_b(hh, dk):
        coff = cvec_ref[hh // 2] * dk
        m_old = m_scr[hh]
        m_new = jnp.maximum(m_old, cmax_scr[hh] + coff)
        alpha_scr[hh] = jnp.exp2(m_old - m_new)
        pt_scr[hh] = jnp.exp2(st_scr[hh] - (m_new - coff)).astype(BF16)
        m_scr[hh] = m_new

    def stage_c(hh, u):
        acc_scr[hh] = acc_scr[hh] * alpha_scr[hh] + _dot(vbuf[u % N_BUF, hh // 2], pt_scr[hh])

    def key_offset(u):
        return jnp.full((1, T_Q), (u - i) * T_S, jnp.int32).astype(F32)

    def body(u, n_sub, stage_a_last):
        for t in range(n_sub):
            wait(u + 1 + t)
        for t in range(n_sub):
            @pl.when(u + N_AHEAD + t <= i)
            def _():
                fetch(u + N_AHEAD + t)
        for t in range(n_sub):
            v = u + t
            dk = key_offset(v)
            vp = jnp.maximum(v - 1, 0)
            stage_a = stage_a_last if t == n_sub - 1 else stage_a_full
            for hh in range(n_hh):
                stage_c(hh, vp)
                stage_b(hh, dk)
                stage_a(hh, v + 1)

    wait(0)

    @pl.when(i == 0)
    def _():
        for hh in range(n_hh):
            stage_a_diag(hh, 0)

    @pl.when(i > 0)
    def _():
        for hh in range(n_hh):
            stage_a_full(hh, 0)

        def loop_body(p, carry):
            body(UNROLL * p, UNROLL, stage_a_full)
            return carry

        n_pairs = (i - 1) // UNROLL
        lax.fori_loop(0, n_pairs, loop_body, 0)

        @pl.when((i - 1) % UNROLL == 1)
        def _():
            body(i - 2, 1, stage_a_full)

        body(i - 1, 1, stage_a_diag)

    dk_last = key_offset(i)
    u_prev = jnp.maximum(i - 1, 0)
    for hh in range(n_hh):
        stage_c(hh, u_prev)
        stage_b(hh, dk_last)
    for hh in range(n_hh):
        stage_c(hh, i)

    def epilogue():
        lv = lamv_ref[...]
        lam = (jnp.exp(jnp.sum(lv[0:1] * lv[1:2], axis=1, keepdims=True))
               - jnp.exp(jnp.sum(lv[2:3] * lv[3:4], axis=1, keepdims=True)) + LAMBDA_INIT)
        for hd in range(DA_HEADS):
            a1 = acc_scr[2 * hd]
            a2 = acc_scr[2 * hd + 1]
            o1 = a1[0:DA_V_DIM] / a1[DA_V_DIM:DA_V_DIM + 1]
            o2 = a2[0:DA_V_DIM] / a2[DA_V_DIM:DA_V_DIM + 1]
            ot = o1 - lam * o2
            inv = lax.rsqrt(jnp.mean(ot * ot, axis=0, keepdims=True) + EPS)
            ot = ot * inv * subg_ref[...] * (1.0 - LAMBDA_INIT)
            ls = slice(hd * DA_V_DIM, (hd + 1) * DA_V_DIM)
            ybuf[:, ls] = ot.T * _silu(z_ref[0, :, ls].astype(F32))
        y = _dot(ybuf[...].astype(BF16), wout_ref[...])
        xo = x_ref[0] + y
        inv = lax.rsqrt(jnp.mean(xo * xo, axis=-1, keepdims=True) + EPS)
        out_ref[0] = xo * inv * fng_ref[...]

    epilogue()


def _np_split3(c):
    import ml_dtypes
    rb = lambda a: a.astype(ml_dtypes.bfloat16).astype(np.float32)
    hi = rb(c)
    mid = rb(c - hi)
    lo = rb(c - hi - mid)
    return hi, mid, lo


def _attn_tables():
    c32 = (np.asarray(ALIBI_SLOPES, np.float64) * LOG2E).astype(np.float32)
    jj = np.arange(T_S, dtype=np.float32)[:, None]
    ii = np.arange(T_Q, dtype=np.float32)[None, :]
    allowed = (jj // CHUNK) <= (ii // CHUNK)
    kaug = np.zeros((T_S, LANES), np.float32)
    kaug[:, 0:3] = jj
    qaug = np.zeros((DA_HEADS, LANES, T_Q), np.float32)
    for r, part in enumerate(_np_split3(c32)):
        qaug[:, r, :] = part[:, None]
    dbias = np.where(allowed[None], c32[:, None, None] * (ii - np.abs(ii - jj))[None], np.float32(NEG))
    cvec = np.broadcast_to(c32[:, None, None], (DA_HEADS, 1, T_Q))
    return (jnp.asarray(kaug, BF16), jnp.asarray(qaug, BF16), jnp.asarray(dbias, F32), jnp.asarray(cvec, F32))


def _layer_c_attn(x, q, k, vt, z, lam_q1, lam_k1, lam_q2, lam_k2, subln_g, w_out, final_g):
    B, L, D = x.shape
    lamv = jnp.zeros((SUBLANES, LANES), F32)
    for r, v in enumerate((lam_q1, lam_k1, lam_q2, lam_k2)):
        lamv = lamv.at[r, :DA_QK_DIM].set(v)
    kaug, qaug, dbias, cvec = _attn_tables()
    cspec = lambda shape: pl.BlockSpec(shape, lambda b, i: (0,) * len(shape), pipeline_mode=pl.Buffered(1))
    row_spec = pl.BlockSpec((1, T_Q, D), lambda b, i: (b, i, 0))
    hbm_spec = pl.BlockSpec(memory_space=pl.ANY)
    in_specs = [row_spec, hbm_spec, hbm_spec, row_spec, row_spec,
                cspec((T_S, LANES)), cspec((DA_HEADS, LANES, T_Q)), cspec((DA_HEADS, T_S, T_Q)),
                cspec((DA_HEADS, 1, T_Q)), cspec((SUBLANES, LANES)), cspec((DA_V_DIM, 1)),
                cspec((1, D)), cspec((DA_INNER, D))]
    return pl.pallas_call(
        _attn_kernel,
        grid=(B, L // T_Q),
        in_specs=in_specs,
        out_specs=row_spec,
        scratch_shapes=_attn_scratch(),
        out_shape=jax.ShapeDtypeStruct((B, L, D), F32),
        compiler_params=pltpu.CompilerParams(
            dimension_semantics=("arbitrary", "arbitrary"), vmem_limit_bytes=VMEM_LIMIT),
        name="layer_c_attn",
    )(q, k, vt, z, x, kaug, qaug, dbias, cvec, lamv, subln_g[:, None], final_g[None, :], w_out.astype(BF16))


def _attn_scratch():
    return [
            pltpu.VMEM((N_BUF, DA_HEADS, T_S, 2 * DA_QK_DIM), BF16),
            pltpu.VMEM((N_BUF, DA_HEADS, V_ROWS, T_S), BF16),
            pltpu.SemaphoreType.DMA((2, N_BUF)),
            pltpu.VMEM((2 * DA_HEADS, 2 * DA_QK_DIM + LANES, T_Q), BF16),
            pltpu.VMEM((2 * DA_HEADS, 1, T_Q), F32),
            pltpu.VMEM((2 * DA_HEADS, V_ROWS, T_Q), F32),
            pltpu.VMEM((2 * DA_HEADS, T_S, T_Q), F32),
            pltpu.VMEM((2 * DA_HEADS, 1, T_Q), F32),
            pltpu.VMEM((2 * DA_HEADS, T_S, T_Q), BF16),
            pltpu.VMEM((2 * DA_HEADS, 1, T_Q), F32),
            pltpu.VMEM((T_Q, DA_INNER), F32),
    ]


def kernel(x, ab_norm_g, ab_w_in, ab_ssd_conv_w, ab_ssd_conv_b, ab_dt_bias, ab_a_log, ab_d_skip, ab_ssd_norm_g,
           ab_ml_conv_w, ab_ml_conv_b, ab_i_bias, ab_f_bias, ab_ml_norm_g, ab_w_out,
           c_norm_g, c_w_in, c_lam_q1, c_lam_k1, c_lam_q2, c_lam_k2, c_subln_g, c_w_out, final_norm_g):
    x1 = _layer_ab(x, ab_norm_g[0], ab_w_in[0], ab_ssd_conv_w[0], ab_ssd_conv_b[0], ab_dt_bias[0], ab_a_log[0],
                   ab_d_skip[0], ab_ssd_norm_g[0], ab_ml_conv_w[0], ab_ml_conv_b[0], ab_i_bias[0], ab_f_bias[0],
                   ab_ml_norm_g[0], ab_w_out[0])
    q, k, vt, z = _layer_c_proj(x1, c_norm_g[0], c_w_in[0])
    return _layer_c_attn(x1, q, k, vt, z, c_lam_q1[0], c_lam_k1[0], c_lam_q2[0], c_lam_k2[0],
                         c_subln_g[0], c_w_out[0], final_norm_g)
```

```python
import functools
import math

import numpy as np
import jax
import jax.numpy as jnp
from jax import lax
from jax.experimental import pallas as pl
from jax.experimental.pallas import tpu as pltpu

F32 = jnp.float32
BF16 = jnp.bfloat16

D_MODEL = 1024
CHUNK = 64
EPS = 1e-6
NEG = -1e30

SSD_HEADS = 16
SSD_HEAD_DIM = 64
SSD_INNER = 1024
SSD_GROUPS = 2
SSD_HPG = 8
SSD_STATE = 128
SSD_GN = 256
SSD_XBC = 1536
SSD_CONV = 4

ML_HEADS = 4
ML_QK_DIM = 128
ML_V_DIM = 256
ML_QK = 512
ML_INNER = 1024
ML_CONV = 4

DA_HEADS = 8
DA_QK_DIM = 64
DA_V_DIM = 128
DA_QK = 1024
DA_INNER = 1024

AB_SIZES = (SSD_INNER, SSD_XBC, SSD_HEADS, ML_INNER, ML_QK, ML_QK, ML_INNER, ML_HEADS, ML_HEADS, ML_INNER)
AB_OFFS = tuple(int(s) for s in np.cumsum((0,) + AB_SIZES))

LANES = 128
SUBLANES = 8

T_AB = 256
SM_DT = 0
SM_IG = 16
SM_FG = 20
VMEM_LIMIT = 56 * 1024 * 1024


def _split3(a):
    hi = a.astype(BF16)
    r = a - hi.astype(F32)
    mid = r.astype(BF16)
    lo = (r - mid.astype(F32)).astype(BF16)
    return hi, mid, lo


def _dot(a, b):
    return jnp.dot(a, b, preferred_element_type=F32)


def _dot_nt(a, b):
    return lax.dot_general(a, b, (((1,), (1,)), ((), ())), preferred_element_type=F32)


def _dot_tn(a, b):
    return lax.dot_general(a, b, (((0,), (0,)), ((), ())), preferred_element_type=F32)


def _sigmoid(v):
    return 1.0 / (1.0 + jnp.exp(-v))


def _silu(v):
    return v * _sigmoid(v)


def _ab_kernel(x_ref, g_ref, wza_ref, wxbc_ref, wzb_ref, wqk_ref, wv_ref, wog_ref, wsm_ref,
               scw_ref, scb_ref, smb_ref, sma_ref, dskip_ref, sng_ref,
               mcw_ref, mcb_ref, mng_ref, wout_ref, tril_ref, e16_ref,
               out_ref,
               xbcp, qkp, sst, cst, nst, mst, ybuf):
    T = T_AB
    c = pl.program_id(1)

    @pl.when(c == 0)
    def _():
        xbcp[0:SUBLANES, :] = jnp.zeros((SUBLANES, SSD_XBC), F32)
        qkp[0:SUBLANES, :] = jnp.zeros((SUBLANES, 2 * ML_QK), F32)
        sst[...] = jnp.zeros_like(sst)
        cst[...] = jnp.zeros_like(cst)
        nst[...] = jnp.zeros_like(nst)
        mst[...] = jnp.zeros_like(mst)

    x = x_ref[0]
    ms = jnp.mean(x * x, axis=-1, keepdims=True)
    h = (x * lax.rsqrt(ms + EPS) * g_ref[...]).astype(BF16)

    def conv(pad_ref, w_ref, cw_ref, cb_ref):
        cur = _dot(h, w_ref[...])
        pad_ref[SUBLANES:SUBLANES + T, :] = cur
        acc = cb_ref[...] + cw_ref[3:4, :] * cur
        for k in range(3):
            off = SUBLANES - 3 + k
            acc = acc + cw_ref[k:k + 1, :] * pad_ref[off:off + T, :]
        pad_ref[0:SUBLANES, :] = pad_ref[T:T + SUBLANES, :]
        return _silu(acc)

    xbc = conv(xbcp, wxbc_ref, scw_ref, scb_ref)
    qk = conv(qkp, wqk_ref, mcw_ref, mcb_ref)

    lane = lax.broadcasted_iota(jnp.int32, (T, LANES), 1)
    sv = _dot(h, wsm_ref[...]) + smb_ref[...]
    e = jnp.log1p(jnp.exp(-jnp.abs(sv)))
    dt = jnp.where(lane < SM_IG, jnp.maximum(sv, 0.0) + e, 0.0)
    fgl = jnp.minimum(sv, 0.0) - e
    cs_in = jnp.where(lane < SM_IG, dt * sma_ref[...],
                      jnp.where(lane < SM_FG, 0.0, jnp.where(lane < SM_FG + ML_HEADS, fgl, 0.0)))
    tril = tril_ref[...]
    c_hi, c_mid, c_lo = _split3(cs_in)
    cs = _dot(tril, c_hi) + _dot(tril, c_mid) + _dot(tril, c_lo)
    scol = jnp.where(lane < SM_IG, cs, jnp.where(lane < SM_FG, sv, cs))
    srow = scol.T
    cs_end = cs[T - 1:T, :]
    eacs = jnp.where(lane < SM_IG, jnp.exp(cs), 0.0)
    toend = jnp.where(lane < SM_IG, jnp.exp(cs_end - cs), 0.0)

    e16 = e16_ref[...]

    def expand(a):
        return _dot(jnp.concatenate(_split3(a), axis=1), e16)

    dt_x = expand(dt)
    eacs_x = expand(eacs)
    toend_x = expand(toend)

    rowi = lax.broadcasted_iota(jnp.int32, (T, T), 0)
    coli = lax.broadcasted_iota(jnp.int32, (T, T), 1)
    causal = rowi >= coli
    lane64 = lane < SSD_HEAD_DIM

    xs = xbc[:, :SSD_INNER]
    xdt = xs * dt_x
    xdt_end = (xdt * toend_x).astype(BF16)
    eend_x = eacs_x[T - 1:T, :]
    za = _dot(h, wza_ref[...])
    gate_a = _silu(za)
    for g in range(SSD_GROUPS):
        bm = xbc[:, SSD_INNER + g * SSD_STATE:SSD_INNER + (g + 1) * SSD_STATE].astype(BF16)
        cm = xbc[:, SSD_INNER + SSD_GN + g * SSD_STATE:SSD_INNER + SSD_GN + (g + 1) * SSD_STATE].astype(BF16)
        cb = _dot_nt(cm, bm)
        gs = slice(g * 512, (g + 1) * 512)
        s_old = sst[g]
        yoff = _dot(cm, s_old.astype(BF16)) * eacs_x[:, gs]
        for p in range(SSD_HPG // 2):
            hd0 = g * SSD_HPG + 2 * p
            ms_ = []
            for hd in (hd0, hd0 + 1):
                seg = scol[:, hd:hd + 1] - srow[hd:hd + 1, :]
                dec = jnp.exp(jnp.where(causal, seg, NEG))
                ms_.append((cb * dec).astype(BF16))
            mcat = jnp.concatenate(ms_, axis=1)
            ls = slice(hd0 * SSD_HEAD_DIM, (hd0 + 2) * SSD_HEAD_DIM)
            slab = xdt[:, ls]
            bd = jnp.concatenate([jnp.where(lane64, slab, 0.0).astype(BF16),
                                  jnp.where(lane64, 0.0, slab).astype(BF16)], axis=0)
            y = _dot(mcat, bd) + yoff[:, 2 * p * SSD_HEAD_DIM:(2 * p + 2) * SSD_HEAD_DIM]
            y = y + xs[:, ls] * dskip_ref[:, ls]
            ybuf[:, ls] = y * gate_a[:, ls]
        sst[g] = s_old * eend_x[:, gs] + _dot_tn(bm, xdt_end[:, gs])
    for g in range(SSD_GROUPS):
        gs = slice(g * 512, (g + 1) * 512)
        yg = ybuf[:, gs]
        inv = lax.rsqrt(jnp.mean(yg * yg, axis=-1, keepdims=True) + EPS)
        ybuf[:, gs] = yg * inv * sng_ref[:, gs]

    vv = _dot(h, wv_ref[...])
    og = _dot(h, wog_ref[...])
    zb = _dot(h, wzb_ref[...])
    for hd in range(ML_HEADS):
        q_h = qk[:, hd * ML_QK_DIM:(hd + 1) * ML_QK_DIM]
        k_h = qk[:, ML_QK + hd * ML_QK_DIM:ML_QK + (hd + 1) * ML_QK_DIM] * (ML_QK_DIM ** -0.5)
        vs = slice(hd * ML_V_DIM, (hd + 1) * ML_V_DIM)
        v_h = vv[:, vs].astype(BF16)
        q_b = q_h.astype(BF16)
        b_c = scol[:, SM_FG + hd:SM_FG + hd + 1]
        b_r = srow[SM_FG + hd:SM_FG + hd + 1, :]
        ig_c = scol[:, SM_IG + hd:SM_IG + hd + 1]
        ig_r = srow[SM_IG + hd:SM_IG + hd + 1, :]
        mp = mst[hd:hd + 1, 0:1]
        c_old = cst[hd]
        n_old = nst[hd:hd + 1, :]
        dlog = jnp.where(causal, b_c - b_r + ig_r, NEG)
        inter = b_c + mp
        m_t = jnp.maximum(inter, jnp.max(dlog, axis=1, keepdims=True))
        w_intra = jnp.exp(dlog - m_t)
        w_inter = jnp.exp(inter - m_t)
        s = _dot_nt(q_b, k_h.astype(BF16)) * w_intra
        num = _dot(s.astype(BF16), v_h) + w_inter * _dot(q_b, c_old.astype(BF16))
        den = jnp.sum(s, axis=1, keepdims=True) + w_inter * jnp.sum(q_h * n_old, axis=1, keepdims=True)
        hh = num / jnp.maximum(jnp.abs(den), jnp.exp(-m_t))
        b_end = b_c[T - 1:T, :]
        glog = b_end - b_c + ig_c
        m_new = jnp.maximum(b_end + mp, jnp.max(glog, axis=0, keepdims=True))
        kg = k_h * jnp.exp(glog - m_new)
        dec = jnp.exp(b_end + mp - m_new)
        cst[hd] = dec * c_old + _dot_tn(kg.astype(BF16), v_h)
        nst[hd:hd + 1, :] = dec * n_old + jnp.sum(kg, axis=0, keepdims=True)
        mst[hd:hd + 1, :] = jnp.broadcast_to(m_new, (1, LANES))
        ho = hh * _sigmoid(og[:, vs])
        inv = lax.rsqrt(jnp.mean(ho * ho, axis=-1, keepdims=True) + EPS)
        ybuf[:, ML_INNER + hd * ML_V_DIM:ML_INNER + (hd + 1) * ML_V_DIM] = (
            ho * inv * mng_ref[:, vs] * _silu(zb[:, vs]))

    y = _dot(ybuf[...].astype(BF16), wout_ref[...])
    out_ref[0] = x + y


def _const_spec(shape):
    nd = len(shape)
    return pl.BlockSpec(shape, lambda b, c: (0,) * nd, pipeline_mode=pl.Buffered(1))


def _layer_ab(x, norm_g, w_in, ssd_cw, ssd_cb, dt_bias, a_log, d_skip, ssd_ng,
              ml_cw, ml_cb, i_bias, f_bias, ml_ng, w_out):
    B, L, D = x.shape
    T = T_AB
    o = AB_OFFS
    seg = lambda i: w_in[:, o[i]:o[i + 1]]
    wza, wxbc, wdt, wzb, wq, wk, wv, wig, wfg, wog = [seg(i) for i in range(10)]
    wsm = jnp.concatenate([wdt, wig, wfg, jnp.zeros((D, LANES - SM_FG - ML_HEADS), F32)], axis=1)
    wqk = jnp.concatenate([wq, wk], axis=1)
    pad = jnp.zeros((LANES - SM_FG - ML_HEADS,), F32)
    smb = jnp.concatenate([dt_bias, i_bias, f_bias, pad])[None, :]
    sma = jnp.concatenate([-jnp.exp(a_log), jnp.zeros((LANES - SSD_HEADS,), F32)])[None, :]
    dskip = jnp.repeat(d_skip, SSD_HEAD_DIM)[None, :]
    tril = jnp.asarray(np.tril(np.ones((T, T), np.float32)), BF16)
    e16_np = np.zeros((LANES, SSD_INNER), np.float32)
    for hd in range(SSD_HEADS):
        e16_np[hd, hd * SSD_HEAD_DIM:(hd + 1) * SSD_HEAD_DIM] = 1.0
    e16 = jnp.asarray(np.concatenate([e16_np] * 3, axis=0), BF16)
    bf = lambda a: a.astype(BF16)
    row = lambda a: a[None, :]
    args = (x, row(norm_g), bf(wza), bf(wxbc), bf(wzb), bf(wqk), bf(wv), bf(wog), bf(wsm),
            ssd_cw, row(ssd_cb), smb, sma, dskip, row(ssd_ng),
            ml_cw, row(ml_cb), row(ml_ng), bf(w_out), tril, e16)
    in_specs = [pl.BlockSpec((1, T, D), lambda b, c: (b, c, 0))]
    in_specs += [_const_spec(a.shape) for a in args[1:]]
    return pl.pallas_call(
        _ab_kernel,
        grid=(B, L // T),
        in_specs=in_specs,
        out_specs=pl.BlockSpec((1, T, D), lambda b, c: (b, c, 0)),
        out_shape=jax.ShapeDtypeStruct((B, L, D), F32),
        scratch_shapes=[
            pltpu.VMEM((T + SUBLANES, SSD_XBC), F32),
            pltpu.VMEM((T + SUBLANES, 2 * ML_QK), F32),
            pltpu.VMEM((SSD_GROUPS, SSD_STATE, SSD_HPG * SSD_HEAD_DIM), F32),
            pltpu.VMEM((ML_HEADS, ML_QK_DIM, ML_V_DIM), F32),
            pltpu.VMEM((SUBLANES, ML_QK_DIM), F32),
            pltpu.VMEM((SUBLANES, LANES), F32),
            pltpu.VMEM((T, SSD_INNER + ML_INNER), F32),
        ],
        compiler_params=pltpu.CompilerParams(
            dimension_semantics=("arbitrary", "arbitrary"), vmem_limit_bytes=VMEM_LIMIT),
        name="layer_ab",
    )(*args)


T_Q = 256
T_S = 256
UNROLL = 4
N_AHEAD = UNROLL + 1
N_BUF = 16
V_ROWS = DA_V_DIM + 16
LAYER_C = 1
LAMBDA_INIT = 0.8 - 0.6 * math.exp(-0.3 * LAYER_C)
ALIBI_SLOPES = tuple(2.0 ** (-8.0 * (h + 1) / DA_HEADS) for h in range(DA_HEADS))
LOG2E = 1.4426950408889634


def _cproj_kernel(x_ref, g_ref, wq_ref, wk_ref, wv_ref, wz_ref, q_ref, k_ref, vt_ref, z_ref):
    x = x_ref[0]
    ms = jnp.mean(x * x, axis=-1, keepdims=True)
    h = (x * lax.rsqrt(ms + EPS) * g_ref[...]).astype(BF16)
    q_ref[0] = _dot(h, wq_ref[...]).astype(BF16)
    z_ref[0] = _dot(h, wz_ref[...]).astype(BF16)
    kk = _dot(h, wk_ref[...])
    vv = _dot(h, wv_ref[...])
    tm = x.shape[0]
    for hd in range(DA_HEADS):
        ls = slice(hd * DA_V_DIM, (hd + 1) * DA_V_DIM)
        k_ref[0, hd] = kk[:, ls].astype(BF16)
        vt_ref[0, hd, 0, 0:DA_V_DIM, :] = vv[:, ls].T.astype(BF16)
        vt_ref[0, hd, 0, DA_V_DIM:V_ROWS, :] = jnp.ones((V_ROWS - DA_V_DIM, tm), BF16)


def _layer_c_proj(x, norm_g, w_in):
    B, L, D = x.shape
    tm = T_S
    wq, wk, wv, wz = [w_in[:, i * DA_QK:(i + 1) * DA_QK].astype(BF16) for i in range(4)]
    cspec = lambda shape: pl.BlockSpec(shape, lambda b, i: (0,) * len(shape), pipeline_mode=pl.Buffered(1))
    return pl.pallas_call(
        _cproj_kernel,
        grid=(B, L // tm),
        in_specs=[pl.BlockSpec((1, tm, D), lambda b, i: (b, i, 0)), cspec((1, D)),
                  cspec((D, DA_QK)), cspec((D, DA_QK)), cspec((D, DA_INNER)), cspec((D, DA_INNER))],
        out_specs=[pl.BlockSpec((1, tm, DA_QK), lambda b, i: (b, i, 0)),
                   pl.BlockSpec((1, DA_HEADS, tm, 2 * DA_QK_DIM), lambda b, i: (b, 0, i, 0)),
                   pl.BlockSpec((1, DA_HEADS, 1, V_ROWS, tm), lambda b, i: (b, 0, i, 0, 0)),
                   pl.BlockSpec((1, tm, DA_INNER), lambda b, i: (b, i, 0))],
        out_shape=[jax.ShapeDtypeStruct((B, L, DA_QK), BF16),
                   jax.ShapeDtypeStruct((B, DA_HEADS, L, 2 * DA_QK_DIM), BF16),
                   jax.ShapeDtypeStruct((B, DA_HEADS, L // tm, V_ROWS, tm), BF16),
                   jax.ShapeDtypeStruct((B, L, DA_INNER), BF16)],
        compiler_params=pltpu.CompilerParams(
            dimension_semantics=("arbitrary", "arbitrary"), vmem_limit_bytes=VMEM_LIMIT),
        name="layer_c_proj",
    )(x, norm_g[None, :], wq, wk, wv, wz)


def _attn_kernel(q_ref, k_hbm, vt_hbm, z_ref, x_ref, kaug_ref, qaug_ref, dbias_ref, cvec_ref, lamv_ref,
                 subg_ref, fng_ref, wout_ref, out_ref, kbuf, vbuf, sem, qt_scr, m_scr, acc_scr, st_scr, cmax_scr,
                 pt_scr, alpha_scr, ybuf):
    b = pl.program_id(0)
    i = pl.program_id(1)
    qk_rows = 2 * DA_QK_DIM

    def kv_copies(u):
        slot = u % N_BUF
        rows = pl.ds(pl.multiple_of(u * T_S, T_S), T_S)
        return (pltpu.make_async_copy(k_hbm.at[b, :, rows, :], kbuf.at[slot], sem.at[0, slot]),
                pltpu.make_async_copy(vt_hbm.at[b, :, u], vbuf.at[slot], sem.at[1, slot]))

    def fetch(u):
        for cp in kv_copies(u):
            cp.start()

    def wait(u):
        for cp in kv_copies(u):
            cp.wait()

    for u in range(N_AHEAD):
        @pl.when(u <= i)
        def _():
            fetch(u)

    def init():
        m_scr[...] = jnp.full(m_scr.shape, NEG, F32)
        acc_scr[...] = jnp.zeros_like(acc_scr)
        pt_scr[...] = jnp.zeros_like(pt_scr)
        alpha_scr[...] = jnp.ones_like(alpha_scr)
        rowi = lax.broadcasted_iota(jnp.int32, (qk_rows, T_Q), 0)
        for hd in range(DA_HEADS):
            qh = q_ref[0, :, hd * qk_rows:(hd + 1) * qk_rows].astype(F32) * (DA_QK_DIM ** -0.5 * LOG2E)
            qht = qh.T
            qt_scr[2 * hd, 0:qk_rows, :] = jnp.where(rowi < DA_QK_DIM, qht, 0.0).astype(BF16)
            qt_scr[2 * hd + 1, 0:qk_rows, :] = jnp.where(rowi < DA_QK_DIM, 0.0, qht).astype(BF16)
            qt_scr[2 * hd, qk_rows:, :] = qaug_ref[hd]
            qt_scr[2 * hd + 1, qk_rows:, :] = qaug_ref[hd]

    init()

    n_hh = 2 * DA_HEADS
    kaug = kaug_ref[...]

    def stage_a_full(hh, u):
        lhs = jnp.concatenate([kbuf[u % N_BUF, hh // 2], kaug], axis=1)
        st = _dot(lhs, qt_scr[hh])
        st_scr[hh] = st
        cmax_scr[hh] = jnp.max(st, axis=0, keepdims=True)

    def stage_a_diag(hh, u):
        st = _dot(kbuf[u % N_BUF, hh // 2], qt_scr[hh, 0:qk_rows, :]) + dbias_ref[hh // 2]
        st_scr[hh] = st
        cmax_scr[hh] = jnp.max(st, axis=0, keepdims=True)

    def stage_b(hh, dk):
        coff = cvec_ref[hh // 2] * dk
        m_old = m_scr[hh]
        m_new = jnp.maximum(m_old, cmax_scr[hh] + coff)
        alpha_scr[hh] = jnp.exp2(m_old - m_new)
        pt_scr[hh] = jnp.exp2(st_scr[hh] - (m_new - coff)).astype(BF16)
        m_scr[hh] = m_new

    def stage_c(hh, u):
        acc_scr[hh] = acc_scr[hh] * alpha_scr[hh] + _dot(vbuf[u % N_BUF, hh // 2], pt_scr[hh])

    def key_offset(u):
        return jnp.full((1, T_Q), (u - i) * T_S, jnp.int32).astype(F32)

    def body(u, n_sub, stage_a_last):
        for t in range(n_sub):
            wait(u + 1 + t)
        for t in range(n_sub):
            @pl.when(u + N_AHEAD + t <= i)
            def _():
                fetch(u + N_AHEAD + t)
        for t in range(n_sub):
            v = u + t
            dk = key_offset(v)
            vp = jnp.maximum(v - 1, 0)
            stage_a = stage_a_last if t == n_sub - 1 else stage_a_full
            for hh in range(n_hh):
                stage_c(hh, vp)
                stage_b(hh, dk)
                stage_a(hh, v + 1)

    wait(0)

    @pl.when(i == 0)
    def _():
        for hh in range(n_hh):
            stage_a_diag(hh, 0)

    @pl.when(i > 0)
    def _():
        for hh in range(n_hh):
            stage_a_full(hh, 0)

        def loop_body(p, carry):
            body(UNROLL * p, UNROLL, stage_a_full)
            return carry

        n_loop = (i - 1) // UNROLL
        lax.fori_loop(0, n_loop, loop_body, 0)
        rem = (i - 1) % UNROLL
        done = n_loop * UNROLL
        piece = UNROLL // 2
        while piece >= 1:
            @pl.when(rem & piece != 0)
            def _(done=done, piece=piece):
                body(done, piece, stage_a_full)
            done = done + jnp.where(rem & piece != 0, piece, 0)
            piece //= 2

        body(i - 1, 1, stage_a_diag)

    dk_last = key_offset(i)
    u_prev = jnp.maximum(i - 1, 0)
    for hh in range(n_hh):
        stage_c(hh, u_prev)
        stage_b(hh, dk_last)
    for hh in range(n_hh):
        stage_c(hh, i)

    def epilogue():
        lv = lamv_ref[...]
        lam = (jnp.exp(jnp.sum(lv[0:1] * lv[1:2], axis=1, keepdims=True))
               - jnp.exp(jnp.sum(lv[2:3] * lv[3:4], axis=1, keepdims=True)) + LAMBDA_INIT)
        for hd in range(DA_HEADS):
            a1 = acc_scr[2 * hd]
            a2 = acc_scr[2 * hd + 1]
            o1 = a1[0:DA_V_DIM] / a1[DA_V_DIM:DA_V_DIM + 1]
            o2 = a2[0:DA_V_DIM] / a2[DA_V_DIM:DA_V_DIM + 1]
            ot = o1 - lam * o2
            inv = lax.rsqrt(jnp.mean(ot * ot, axis=0, keepdims=True) + EPS)
            ot = ot * inv * subg_ref[...] * (1.0 - LAMBDA_INIT)
            ls = slice(hd * DA_V_DIM, (hd + 1) * DA_V_DIM)
            ybuf[:, ls] = ot.T * _silu(z_ref[0, :, ls].astype(F32))
        y = _dot(ybuf[...].astype(BF16), wout_ref[...])
        xo = x_ref[0] + y
        inv = lax.rsqrt(jnp.mean(xo * xo, axis=-1, keepdims=True) + EPS)
        out_ref[0] = xo * inv * fng_ref[...]

    epilogue()


def _np_split3(c):
    import ml_dtypes
    rb = lambda a: a.astype(ml_dtypes.bfloat16).astype(np.float32)
    hi = rb(c)
    mid = rb(c - hi)
    lo = rb(c - hi - mid)
    return hi, mid, lo


def _attn_tables():
    c32 = (np.asarray(ALIBI_SLOPES, np.float64) * LOG2E).astype(np.float32)
    jj = np.arange(T_S, dtype=np.float32)[:, None]
    ii = np.arange(T_Q, dtype=np.float32)[None, :]
    allowed = (jj // CHUNK) <= (ii // CHUNK)
    kaug = np.zeros((T_S, LANES), np.float32)
    kaug[:, 0:3] = jj
    qaug = np.zeros((DA_HEADS, LANES, T_Q), np.float32)
    for r, part in enumerate(_np_split3(c32)):
        qaug[:, r, :] = part[:, None]
    dbias = np.where(allowed[None], c32[:, None, None] * (ii - np.abs(ii - jj))[None], np.float32(NEG))
    cvec = np.broadcast_to(c32[:, None, None], (DA_HEADS, 1, T_Q))
    return (jnp.asarray(kaug, BF16), jnp.asarray(qaug, BF16), jnp.asarray(dbias, F32), jnp.asarray(cvec, F32))


def _layer_c_attn(x, q, k, vt, z, lam_q1, lam_k1, lam_q2, lam_k2, subln_g, w_out, final_g):
    B, L, D = x.shape
    lamv = jnp.zeros((SUBLANES, LANES), F32)
    for r, v in enumerate((lam_q1, lam_k1, lam_q2, lam_k2)):
        lamv = lamv.at[r, :DA_QK_DIM].set(v)
    kaug, qaug, dbias, cvec = _attn_tables()
    cspec = lambda shape: pl.BlockSpec(shape, lambda b, i: (0,) * len(shape), pipeline_mode=pl.Buffered(1))
    row_spec = pl.BlockSpec((1, T_Q, D), lambda b, i: (b, i, 0))
    hbm_spec = pl.BlockSpec(memory_space=pl.ANY)
    in_specs = [row_spec, hbm_spec, hbm_spec, row_spec, row_spec,
                cspec((T_S, LANES)), cspec((DA_HEADS, LANES, T_Q)), cspec((DA_HEADS, T_S, T_Q)),
                cspec((DA_HEADS, 1, T_Q)), cspec((SUBLANES, LANES)), cspec((DA_V_DIM, 1)),
                cspec((1, D)), cspec((DA_INNER, D))]
    return pl.pallas_call(
        _attn_kernel,
        grid=(B, L // T_Q),
        in_specs=in_specs,
        out_specs=row_spec,
        scratch_shapes=_attn_scratch(),
        out_shape=jax.ShapeDtypeStruct((B, L, D), F32),
        compiler_params=pltpu.CompilerParams(
            dimension_semantics=("arbitrary", "arbitrary"), vmem_limit_bytes=VMEM_LIMIT),
        name="layer_c_attn",
    )(q, k, vt, z, x, kaug, qaug, dbias, cvec, lamv, subln_g[:, None], final_g[None, :], w_out.astype(BF16))


def _attn_scratch():
    return [
            pltpu.VMEM((N_BUF, DA_HEADS, T_S, 2 * DA_QK_DIM), BF16),
            pltpu.VMEM((N_BUF, DA_HEADS, V_ROWS, T_S), BF16),
            pltpu.SemaphoreType.DMA((2, N_BUF)),
            pltpu.VMEM((2 * DA_HEADS, 2 * DA_QK_DIM + LANES, T_Q), BF16),
            pltpu.VMEM((2 * DA_HEADS, 1, T_Q), F32),
            pltpu.VMEM((2 * DA_HEADS, V_ROWS, T_Q), F32),
            pltpu.VMEM((2 * DA_HEADS, T_S, T_Q), F32),
            pltpu.VMEM((2 * DA_HEADS, 1, T_Q), F32),
            pltpu.VMEM((2 * DA_HEADS, T_S, T_Q), BF16),
            pltpu.VMEM((2 * DA_HEADS, 1, T_Q), F32),
            pltpu.VMEM((T_Q, DA_INNER), F32),
    ]


def kernel(x, ab_norm_g, ab_w_in, ab_ssd_conv_w, ab_ssd_conv_b, ab_dt_bias, ab_a_log, ab_d_skip, ab_ssd_norm_g,
           ab_ml_conv_w, ab_ml_conv_b, ab_i_bias, ab_f_bias, ab_ml_norm_g, ab_w_out,
           c_norm_g, c_w_in, c_lam_q1, c_lam_k1, c_lam_q2, c_lam_k2, c_subln_g, c_w_out, final_norm_g):
    x1 = _layer_ab(x, ab_norm_g[0], ab_w_in[0], ab_ssd_conv_w[0], ab_ssd_conv_b[0], ab_dt_bias[0], ab_a_log[0],
                   ab_d_skip[0], ab_ssd_norm_g[0], ab_ml_conv_w[0], ab_ml_conv_b[0], ab_i_bias[0], ab_f_bias[0],
                   ab_ml_norm_g[0], ab_w_out[0])
    q, k, vt, z = _layer_c_proj(x1, c_norm_g[0], c_w_in[0])
    return _layer_c_attn(x1, q, k, vt, z, c_lam_q1[0], c_lam_k1[0], c_lam_q2[0], c_lam_k2[0],
                         c_subln_g[0], c_w_out[0], final_norm_g)
```

```python
import functools
import math

import numpy as np
import jax
import jax.numpy as jnp
from jax import lax
from jax.experimental import pallas as pl
from jax.experimental.pallas import tpu as pltpu

F32 = jnp.float32
BF16 = jnp.bfloat16

D_MODEL = 1024
CHUNK = 64
EPS = 1e-6
NEG = -1e30

SSD_HEADS = 16
SSD_HEAD_DIM = 64
SSD_INNER = 1024
SSD_GROUPS = 2
SSD_HPG = 8
SSD_STATE = 128
SSD_GN = 256
SSD_XBC = 1536
SSD_CONV = 4

ML_HEADS = 4
ML_QK_DIM = 128
ML_V_DIM = 256
ML_QK = 512
ML_INNER = 1024
ML_CONV = 4

DA_HEADS = 8
DA_QK_DIM = 64
DA_V_DIM = 128
DA_QK = 1024
DA_INNER = 1024

AB_SIZES = (SSD_INNER, SSD_XBC, SSD_HEADS, ML_INNER, ML_QK, ML_QK, ML_INNER, ML_HEADS, ML_HEADS, ML_INNER)
AB_OFFS = tuple(int(s) for s in np.cumsum((0,) + AB_SIZES))

LANES = 128
SUBLANES = 8

T_AB = 256
SM_DT = 0
SM_IG = 16
SM_FG = 20
VMEM_LIMIT = 56 * 1024 * 1024


def _split3(a):
    hi = a.astype(BF16)
    r = a - hi.astype(F32)
    mid = r.astype(BF16)
    lo = (r - mid.astype(F32)).astype(BF16)
    return hi, mid, lo


def _dot(a, b):
    return jnp.dot(a, b, preferred_element_type=F32)


def _dot_nt(a, b):
    return lax.dot_general(a, b, (((1,), (1,)), ((), ())), preferred_element_type=F32)


def _dot_tn(a, b):
    return lax.dot_general(a, b, (((0,), (0,)), ((), ())), preferred_element_type=F32)


def _sigmoid(v):
    return 1.0 / (1.0 + jnp.exp(-v))


def _silu(v):
    return v * _sigmoid(v)


def _ab_kernel(x_ref, g_ref, wza_ref, wxbc_ref, wzb_ref, wqk_ref, wv_ref, wog_ref, wsm_ref,
               scw_ref, scb_ref, smb_ref, sma_ref, dskip_ref, sng_ref,
               mcw_ref, mcb_ref, mng_ref, wout_ref, tril_ref, e16_ref,
               out_ref,
               xbcp, qkp, sst, cst, nst, mst, ybuf):
    T = T_AB
    c = pl.program_id(1)

    @pl.when(c == 0)
    def _():
        xbcp[0:SUBLANES, :] = jnp.zeros((SUBLANES, SSD_XBC), F32)
        qkp[0:SUBLANES, :] = jnp.zeros((SUBLANES, 2 * ML_QK), F32)
        sst[...] = jnp.zeros_like(sst)
        cst[...] = jnp.zeros_like(cst)
        nst[...] = jnp.zeros_like(nst)
        mst[...] = jnp.zeros_like(mst)

    x = x_ref[0]
    ms = jnp.mean(x * x, axis=-1, keepdims=True)
    h = (x * lax.rsqrt(ms + EPS) * g_ref[...]).astype(BF16)

    def conv(pad_ref, w_ref, cw_ref, cb_ref):
        cur = _dot(h, w_ref[...])
        pad_ref[SUBLANES:SUBLANES + T, :] = cur
        acc = cb_ref[...] + cw_ref[3:4, :] * cur
        for k in range(3):
            off = SUBLANES - 3 + k
            acc = acc + cw_ref[k:k + 1, :] * pad_ref[off:off + T, :]
        pad_ref[0:SUBLANES, :] = pad_ref[T:T + SUBLANES, :]
        return _silu(acc)

    xbc = conv(xbcp, wxbc_ref, scw_ref, scb_ref)
    qk = conv(qkp, wqk_ref, mcw_ref, mcb_ref)

    lane = lax.broadcasted_iota(jnp.int32, (T, LANES), 1)
    sv = _dot(h, wsm_ref[...]) + smb_ref[...]
    e = jnp.log1p(jnp.exp(-jnp.abs(sv)))
    dt = jnp.where(lane < SM_IG, jnp.maximum(sv, 0.0) + e, 0.0)
    fgl = jnp.minimum(sv, 0.0) - e
    cs_in = jnp.where(lane < SM_IG, dt * sma_ref[...],
                      jnp.where(lane < SM_FG, 0.0, jnp.where(lane < SM_FG + ML_HEADS, fgl, 0.0)))
    tril = tril_ref[...]
    c_hi, c_mid, c_lo = _split3(cs_in)
    cs = _dot(tril, c_hi) + _dot(tril, c_mid) + _dot(tril, c_lo)
    scol = jnp.where(lane < SM_IG, cs, jnp.where(lane < SM_FG, sv, cs))
    srow = scol.T
    cs_end = cs[T - 1:T, :]
    eacs = jnp.where(lane < SM_IG, jnp.exp(cs), 0.0)
    toend = jnp.where(lane < SM_IG, jnp.exp(cs_end - cs), 0.0)

    e16 = e16_ref[...]

    def expand(a):
        return _dot(jnp.concatenate(_split3(a), axis=1), e16)

    dt_x = expand(dt)
    eacs_x = expand(eacs)
    toend_x = expand(toend)

    rowi = lax.broadcasted_iota(jnp.int32, (T, T), 0)
    coli = lax.broadcasted_iota(jnp.int32, (T, T), 1)
    causal = rowi >= coli
    lane64 = lane < SSD_HEAD_DIM

    xs = xbc[:, :SSD_INNER]
    xdt = xs * dt_x
    xdt_end = (xdt * toend_x).astype(BF16)
    eend_x = eacs_x[T - 1:T, :]
    za = _dot(h, wza_ref[...])
    gate_a = _silu(za)
    for g in range(SSD_GROUPS):
        bm = xbc[:, SSD_INNER + g * SSD_STATE:SSD_INNER + (g + 1) * SSD_STATE].astype(BF16)
        cm = xbc[:, SSD_INNER + SSD_GN + g * SSD_STATE:SSD_INNER + SSD_GN + (g + 1) * SSD_STATE].astype(BF16)
        cb = _dot_nt(cm, bm)
        gs = slice(g * 512, (g + 1) * 512)
        s_old = sst[g]
        yoff = _dot(cm, s_old.astype(BF16)) * eacs_x[:, gs]
        for p in range(SSD_HPG // 2):
            hd0 = g * SSD_HPG + 2 * p
            ms_ = []
            for hd in (hd0, hd0 + 1):
                seg = scol[:, hd:hd + 1] - srow[hd:hd + 1, :]
                dec = jnp.exp(jnp.where(causal, seg, NEG))
                ms_.append((cb * dec).astype(BF16))
            mcat = jnp.concatenate(ms_, axis=1)
            ls = slice(hd0 * SSD_HEAD_DIM, (hd0 + 2) * SSD_HEAD_DIM)
            slab = xdt[:, ls]
            bd = jnp.concatenate([jnp.where(lane64, slab, 0.0).astype(BF16),
                                  jnp.where(lane64, 0.0, slab).astype(BF16)], axis=0)
            y = _dot(mcat, bd) + yoff[:, 2 * p * SSD_HEAD_DIM:(2 * p + 2) * SSD_HEAD_DIM]
            y = y + xs[:, ls] * dskip_ref[:, ls]
            ybuf[:, ls] = y * gate_a[:, ls]
        sst[g] = s_old * eend_x[:, gs] + _dot_tn(bm, xdt_end[:, gs])
    for g in range(SSD_GROUPS):
        gs = slice(g * 512, (g + 1) * 512)
        yg = ybuf[:, gs]
        inv = lax.rsqrt(jnp.mean(yg * yg, axis=-1, keepdims=True) + EPS)
        ybuf[:, gs] = yg * inv * sng_ref[:, gs]

    vv = _dot(h, wv_ref[...])
    og = _dot(h, wog_ref[...])
    zb = _dot(h, wzb_ref[...])
    for hd in range(ML_HEADS):
        q_h = qk[:, hd * ML_QK_DIM:(hd + 1) * ML_QK_DIM]
        k_h = qk[:, ML_QK + hd * ML_QK_DIM:ML_QK + (hd + 1) * ML_QK_DIM] * (ML_QK_DIM ** -0.5)
        vs = slice(hd * ML_V_DIM, (hd + 1) * ML_V_DIM)
        v_h = vv[:, vs].astype(BF16)
        q_b = q_h.astype(BF16)
        b_c = scol[:, SM_FG + hd:SM_FG + hd + 1]
        b_r = srow[SM_FG + hd:SM_FG + hd + 1, :]
        ig_c = scol[:, SM_IG + hd:SM_IG + hd + 1]
        ig_r = srow[SM_IG + hd:SM_IG + hd + 1, :]
        mp = mst[hd:hd + 1, 0:1]
        c_old = cst[hd]
        n_old = nst[hd:hd + 1, :]
        dlog = jnp.where(causal, b_c - b_r + ig_r, NEG)
        inter = b_c + mp
        m_t = jnp.maximum(inter, jnp.max(dlog, axis=1, keepdims=True))
        w_intra = jnp.exp(dlog - m_t)
        w_inter = jnp.exp(inter - m_t)
        s = _dot_nt(q_b, k_h.astype(BF16)) * w_intra
        num = _dot(s.astype(BF16), v_h) + w_inter * _dot(q_b, c_old.astype(BF16))
        den = jnp.sum(s, axis=1, keepdims=True) + w_inter * jnp.sum(q_h * n_old, axis=1, keepdims=True)
        hh = num / jnp.maximum(jnp.abs(den), jnp.exp(-m_t))
        b_end = b_c[T - 1:T, :]
        glog = b_end - b_c + ig_c
        m_new = jnp.maximum(b_end + mp, jnp.max(glog, axis=0, keepdims=True))
        kg = k_h * jnp.exp(glog - m_new)
        dec = jnp.exp(b_end + mp - m_new)
        cst[hd] = dec * c_old + _dot_tn(kg.astype(BF16), v_h)
        nst[hd:hd + 1, :] = dec * n_old + jnp.sum(kg, axis=0, keepdims=True)
        mst[hd:hd + 1, :] = jnp.broadcast_to(m_new, (1, LANES))
        ho = hh * _sigmoid(og[:, vs])
        inv = lax.rsqrt(jnp.mean(ho * ho, axis=-1, keepdims=True) + EPS)
        ybuf[:, ML_INNER + hd * ML_V_DIM:ML_INNER + (hd + 1) * ML_V_DIM] = (
            ho * inv * mng_ref[:, vs] * _silu(zb[:, vs]))

    y = _dot(ybuf[...].astype(BF16), wout_ref[...])
    out_ref[0] = x + y


def _const_spec(shape):
    nd = len(shape)
    return pl.BlockSpec(shape, lambda b, c: (0,) * nd, pipeline_mode=pl.Buffered(1))


def _layer_ab(x, norm_g, w_in, ssd_cw, ssd_cb, dt_bias, a_log, d_skip, ssd_ng,
              ml_cw, ml_cb, i_bias, f_bias, ml_ng, w_out):
    B, L, D = x.shape
    T = T_AB
    o = AB_OFFS
    seg = lambda i: w_in[:, o[i]:o[i + 1]]
    wza, wxbc, wdt, wzb, wq, wk, wv, wig, wfg, wog = [seg(i) for i in range(10)]
    wsm = jnp.concatenate([wdt, wig, wfg, jnp.zeros((D, LANES - SM_FG - ML_HEADS), F32)], axis=1)
    wqk = jnp.concatenate([wq, wk], axis=1)
    pad = jnp.zeros((LANES - SM_FG - ML_HEADS,), F32)
    smb = jnp.concatenate([dt_bias, i_bias, f_bias, pad])[None, :]
    sma = jnp.concatenate([-jnp.exp(a_log), jnp.zeros((LANES - SSD_HEADS,), F32)])[None, :]
    dskip = jnp.repeat(d_skip, SSD_HEAD_DIM)[None, :]
    tril = jnp.asarray(np.tril(np.ones((T, T), np.float32)), BF16)
    e16_np = np.zeros((LANES, SSD_INNER), np.float32)
    for hd in range(SSD_HEADS):
        e16_np[hd, hd * SSD_HEAD_DIM:(hd + 1) * SSD_HEAD_DIM] = 1.0
    e16 = jnp.asarray(np.concatenate([e16_np] * 3, axis=0), BF16)
    bf = lambda a: a.astype(BF16)
    row = lambda a: a[None, :]
    args = (x, row(norm_g), bf(wza), bf(wxbc), bf(wzb), bf(wqk), bf(wv), bf(wog), bf(wsm),
            ssd_cw, row(ssd_cb), smb, sma, dskip, row(ssd_ng),
            ml_cw, row(ml_cb), row(ml_ng), bf(w_out), tril, e16)
    in_specs = [pl.BlockSpec((1, T, D), lambda b, c: (b, c, 0))]
    in_specs += [_const_spec(a.shape) for a in args[1:]]
    return pl.pallas_call(
        _ab_kernel,
        grid=(B, L // T),
        in_specs=in_specs,
        out_specs=pl.BlockSpec((1, T, D), lambda b, c: (b, c, 0)),
        out_shape=jax.ShapeDtypeStruct((B, L, D), F32),
        scratch_shapes=[
            pltpu.VMEM((T + SUBLANES, SSD_XBC), F32),
            pltpu.VMEM((T + SUBLANES, 2 * ML_QK), F32),
            pltpu.VMEM((SSD_GROUPS, SSD_STATE, SSD_HPG * SSD_HEAD_DIM), F32),
            pltpu.VMEM((ML_HEADS, ML_QK_DIM, ML_V_DIM), F32),
            pltpu.VMEM((SUBLANES, ML_QK_DIM), F32),
            pltpu.VMEM((SUBLANES, LANES), F32),
            pltpu.VMEM((T, SSD_INNER + ML_INNER), F32),
        ],
        compiler_params=pltpu.CompilerParams(
            dimension_semantics=("arbitrary", "arbitrary"), vmem_limit_bytes=VMEM_LIMIT),
        name="layer_ab",
    )(*args)


T_Q = 256
T_S = 256
UNROLL = 4
N_AHEAD = UNROLL + 1
N_BUF = 16
V_ROWS = DA_V_DIM + 16
LAYER_C = 1
LAMBDA_INIT = 0.8 - 0.6 * math.exp(-0.3 * LAYER_C)
ALIBI_SLOPES = tuple(2.0 ** (-8.0 * (h + 1) / DA_HEADS) for h in range(DA_HEADS))
LOG2E = 1.4426950408889634


def _cproj_kernel(x_ref, g_ref, wq_ref, wk_ref, wv_ref, wz_ref, q_ref, k_ref, vt_ref, z_ref):
    x = x_ref[0]
    ms = jnp.mean(x * x, axis=-1, keepdims=True)
    h = (x * lax.rsqrt(ms + EPS) * g_ref[...]).astype(BF16)
    q_ref[0] = _dot(h, wq_ref[...]).astype(BF16)
    z_ref[0] = _dot(h, wz_ref[...]).astype(BF16)
    kk = _dot(h, wk_ref[...])
    vv = _dot(h, wv_ref[...])
    tm = x.shape[0]
    for hd in range(DA_HEADS):
        ls = slice(hd * DA_V_DIM, (hd + 1) * DA_V_DIM)
        k_ref[0, hd] = kk[:, ls].astype(BF16)
        vt_ref[0, hd, 0, 0:DA_V_DIM, :] = vv[:, ls].T.astype(BF16)
        vt_ref[0, hd, 0, DA_V_DIM:V_ROWS, :] = jnp.ones((V_ROWS - DA_V_DIM, tm), BF16)


def _layer_c_proj(x, norm_g, w_in):
    B, L, D = x.shape
    tm = T_S
    wq, wk, wv, wz = [w_in[:, i * DA_QK:(i + 1) * DA_QK].astype(BF16) for i in range(4)]
    cspec = lambda shape: pl.BlockSpec(shape, lambda b, i: (0,) * len(shape), pipeline_mode=pl.Buffered(1))
    return pl.pallas_call(
        _cproj_kernel,
        grid=(B, L // tm),
        in_specs=[pl.BlockSpec((1, tm, D), lambda b, i: (b, i, 0)), cspec((1, D)),
                  cspec((D, DA_QK)), cspec((D, DA_QK)), cspec((D, DA_INNER)), cspec((D, DA_INNER))],
        out_specs=[pl.BlockSpec((1, tm, DA_QK), lambda b, i: (b, i, 0)),
                   pl.BlockSpec((1, DA_HEADS, tm, 2 * DA_QK_DIM), lambda b, i: (b, 0, i, 0)),
                   pl.BlockSpec((1, DA_HEADS, 1, V_ROWS, tm), lambda b, i: (b, 0, i, 0, 0)),
                   pl.BlockSpec((1, tm, DA_INNER), lambda b, i: (b, i, 0))],
        out_shape=[jax.ShapeDtypeStruct((B, L, DA_QK), BF16),
                   jax.ShapeDtypeStruct((B, DA_HEADS, L, 2 * DA_QK_DIM), BF16),
                   jax.ShapeDtypeStruct((B, DA_HEADS, L // tm, V_ROWS, tm), BF16),
                   jax.ShapeDtypeStruct((B, L, DA_INNER), BF16)],
        compiler_params=pltpu.CompilerParams(
            dimension_semantics=("arbitrary", "arbitrary"), vmem_limit_bytes=VMEM_LIMIT),
        name="layer_c_proj",
    )(x, norm_g[None, :], wq, wk, wv, wz)


def _attn_kernel(q_ref, k_hbm, vt_hbm, z_ref, x_ref, kaug_ref, qaug_ref, dbias_ref, cvec_ref, lamv_ref,
                 subg_ref, fng_ref, wout_ref, out_ref, kbuf, vbuf, sem, qt_scr, m_scr, acc_scr, st_scr, cmax_scr,
                 pt_scr, alpha_scr, ybuf):
    b = pl.program_id(0)
    i = pl.program_id(1)
    qk_rows = 2 * DA_QK_DIM

    def kv_copies(u, bb=b):
        slot = u % N_BUF
        rows = pl.ds(pl.multiple_of(u * T_S, T_S), T_S)
        return (pltpu.make_async_copy(k_hbm.at[bb, :, rows, :], kbuf.at[slot], sem.at[0, slot]),
                pltpu.make_async_copy(vt_hbm.at[bb, :, u], vbuf.at[slot], sem.at[1, slot]))

    def fetch(u, bb=b):
        for cp in kv_copies(u, bb):
            cp.start()

    def wait(u):
        for cp in kv_copies(u):
            cp.wait()

    def fetch_head(bb, ii):
        for u in range(N_AHEAD):
            @pl.when(u <= ii)
            def _():
                fetch(u, bb)

    @pl.when((b == 0) & (i == 0))
    def _():
        fetch_head(b, i)

    def init():
        m_scr[...] = jnp.full(m_scr.shape, NEG, F32)
        acc_scr[...] = jnp.zeros_like(acc_scr)
        pt_scr[...] = jnp.zeros_like(pt_scr)
        alpha_scr[...] = jnp.ones_like(alpha_scr)
        rowi = lax.broadcasted_iota(jnp.int32, (qk_rows, T_Q), 0)
        for hd in range(DA_HEADS):
            qh = q_ref[0, :, hd * qk_rows:(hd + 1) * qk_rows].astype(F32) * (DA_QK_DIM ** -0.5 * LOG2E)
            qht = qh.T
            qt_scr[2 * hd, 0:qk_rows, :] = jnp.where(rowi < DA_QK_DIM, qht, 0.0).astype(BF16)
            qt_scr[2 * hd + 1, 0:qk_rows, :] = jnp.where(rowi < DA_QK_DIM, 0.0, qht).astype(BF16)
            qt_scr[2 * hd, qk_rows:, :] = qaug_ref[hd]
            qt_scr[2 * hd + 1, qk_rows:, :] = qaug_ref[hd]

    init()

    n_hh = 2 * DA_HEADS
    kaug = kaug_ref[...]

    def stage_a_full(hh, u):
        lhs = jnp.concatenate([kbuf[u % N_BUF, hh // 2], kaug], axis=1)
        st = _dot(lhs, qt_scr[hh])
        st_scr[hh] = st
        cmax_scr[hh] = jnp.max(st, axis=0, keepdims=True)

    def stage_a_diag(hh, u):
        st = _dot(kbuf[u % N_BUF, hh // 2], qt_scr[hh, 0:qk_rows, :]) + dbias_ref[hh // 2]
        st_scr[hh] = st
        cmax_scr[hh] = jnp.max(st, axis=0, keepdims=True)

    def stage_b(hh, dk):
        coff = cvec_ref[hh // 2] * dk
        m_old = m_scr[hh]
        m_new = jnp.maximum(m_old, cmax_scr[hh] + coff)
        alpha_scr[hh] = jnp.exp2(m_old - m_new)
        pt_scr[hh] = jnp.exp2(st_scr[hh] - (m_new - coff)).astype(BF16)
        m_scr[hh] = m_new

    def stage_c(hh, u):
        acc_scr[hh] = acc_scr[hh] * alpha_scr[hh] + _dot(vbuf[u % N_BUF, hh // 2], pt_scr[hh])

    def key_offset(u):
        return jnp.full((1, T_Q), (u - i) * T_S, jnp.int32).astype(F32)

    def body(u, n_sub, stage_a_last):
        for t in range(n_sub):
            wait(u + 1 + t)
        for t in range(n_sub):
            @pl.when(u + N_AHEAD + t <= i)
            def _():
                fetch(u + N_AHEAD + t)
        for t in range(n_sub):
            v = u + t
            dk = key_offset(v)
            vp = jnp.maximum(v - 1, 0)
            stage_a = stage_a_last if t == n_sub - 1 else stage_a_full
            for hh in range(n_hh):
                stage_c(hh, vp)
                stage_b(hh, dk)
                stage_a(hh, v + 1)

    wait(0)

    @pl.when(i == 0)
    def _():
        for hh in range(n_hh):
            stage_a_diag(hh, 0)

    @pl.when(i > 0)
    def _():
        for hh in range(n_hh):
            stage_a_full(hh, 0)

        def loop_body(p, carry):
            body(UNROLL * p, UNROLL, stage_a_full)
            return carry

        n_loop = (i - 1) // UNROLL
        lax.fori_loop(0, n_loop, loop_body, 0)
        rem = (i - 1) % UNROLL
        done = n_loop * UNROLL
        piece = UNROLL // 2
        while piece >= 1:
            @pl.when(rem & piece != 0)
            def _(done=done, piece=piece):
                body(done, piece, stage_a_full)
            done = done + jnp.where(rem & piece != 0, piece, 0)
            piece //= 2

        body(i - 1, 1, stage_a_diag)

    dk_last = key_offset(i)
    u_prev = jnp.maximum(i - 1, 0)
    for hh in range(n_hh):
        stage_c(hh, u_prev)
        stage_b(hh, dk_last)
    for hh in range(n_hh):
        stage_c(hh, i)

    n_b, n_i = pl.num_programs(0), pl.num_programs(1)
    wrap = i + 1 == n_i
    next_b = jnp.where(wrap, b + 1, b)
    next_i = jnp.where(wrap, 0, i + 1)

    @pl.when(next_b < n_b)
    def _():
        fetch_head(next_b, next_i)

    def epilogue():
        lv = lamv_ref[...]
        lam = (jnp.exp(jnp.sum(lv[0:1] * lv[1:2], axis=1, keepdims=True))
               - jnp.exp(jnp.sum(lv[2:3] * lv[3:4], axis=1, keepdims=True)) + LAMBDA_INIT)
        for hd in range(DA_HEADS):
            a1 = acc_scr[2 * hd]
            a2 = acc_scr[2 * hd + 1]
            o1 = a1[0:DA_V_DIM] / a1[DA_V_DIM:DA_V_DIM + 1]
            o2 = a2[0:DA_V_DIM] / a2[DA_V_DIM:DA_V_DIM + 1]
            ot = o1 - lam * o2
            inv = lax.rsqrt(jnp.mean(ot * ot, axis=0, keepdims=True) + EPS)
            ot = ot * inv * subg_ref[...] * (1.0 - LAMBDA_INIT)
            ls = slice(hd * DA_V_DIM, (hd + 1) * DA_V_DIM)
            ybuf[:, ls] = ot.T * _silu(z_ref[0, :, ls].astype(F32))
        y = _dot(ybuf[...].astype(BF16), wout_ref[...])
        xo = x_ref[0] + y
        inv = lax.rsqrt(jnp.mean(xo * xo, axis=-1, keepdims=True) + EPS)
        out_ref[0] = xo * inv * fng_ref[...]

    epilogue()


def _np_split3(c):
    import ml_dtypes
    rb = lambda a: a.astype(ml_dtypes.bfloat16).astype(np.float32)
    hi = rb(c)
    mid = rb(c - hi)
    lo = rb(c - hi - mid)
    return hi, mid, lo


def _attn_tables():
    c32 = (np.asarray(ALIBI_SLOPES, np.float64) * LOG2E).astype(np.float32)
    jj = np.arange(T_S, dtype=np.float32)[:, None]
    ii = np.arange(T_Q, dtype=np.float32)[None, :]
    allowed = (jj // CHUNK) <= (ii // CHUNK)
    kaug = np.zeros((T_S, LANES), np.float32)
    kaug[:, 0:3] = jj
    qaug = np.zeros((DA_HEADS, LANES, T_Q), np.float32)
    for r, part in enumerate(_np_split3(c32)):
        qaug[:, r, :] = part[:, None]
    dbias = np.where(allowed[None], c32[:, None, None] * (ii - np.abs(ii - jj))[None], np.float32(NEG))
    cvec = np.broadcast_to(c32[:, None, None], (DA_HEADS, 1, T_Q))
    return (jnp.asarray(kaug, BF16), jnp.asarray(qaug, BF16), jnp.asarray(dbias, F32), jnp.asarray(cvec, F32))


def _layer_c_attn(x, q, k, vt, z, lam_q1, lam_k1, lam_q2, lam_k2, subln_g, w_out, final_g):
    B, L, D = x.shape
    lamv = jnp.zeros((SUBLANES, LANES), F32)
    for r, v in enumerate((lam_q1, lam_k1, lam_q2, lam_k2)):
        lamv = lamv.at[r, :DA_QK_DIM].set(v)
    kaug, qaug, dbias, cvec = _attn_tables()
    cspec = lambda shape: pl.BlockSpec(shape, lambda b, i: (0,) * len(shape), pipeline_mode=pl.Buffered(1))
    row_spec = pl.BlockSpec((1, T_Q, D), lambda b, i: (b, i, 0))
    hbm_spec = pl.BlockSpec(memory_space=pl.ANY)
    in_specs = [row_spec, hbm_spec, hbm_spec, row_spec, row_spec,
                cspec((T_S, LANES)), cspec((DA_HEADS, LANES, T_Q)), cspec((DA_HEADS, T_S, T_Q)),
                cspec((DA_HEADS, 1, T_Q)), cspec((SUBLANES, LANES)), cspec((DA_V_DIM, 1)),
                cspec((1, D)), cspec((DA_INNER, D))]
    return pl.pallas_call(
        _attn_kernel,
        grid=(B, L // T_Q),
        in_specs=in_specs,
        out_specs=row_spec,
        scratch_shapes=_attn_scratch(),
        out_shape=jax.ShapeDtypeStruct((B, L, D), F32),
        compiler_params=pltpu.CompilerParams(
            dimension_semantics=("arbitrary", "arbitrary"), vmem_limit_bytes=VMEM_LIMIT),
        name="layer_c_attn",
    )(q, k, vt, z, x, kaug, qaug, dbias, cvec, lamv, subln_g[:, None], final_g[None, :], w_out.astype(BF16))


def _attn_scratch():
    return [
            pltpu.VMEM((N_BUF, DA_HEADS, T_S, 2 * DA_QK_DIM), BF16),
            pltpu.VMEM((N_BUF, DA_HEADS, V_ROWS, T_S), BF16),
            pltpu.SemaphoreType.DMA((2, N_BUF)),
            pltpu.VMEM((2 * DA_HEADS, 2 * DA_QK_DIM + LANES, T_Q), BF16),
            pltpu.VMEM((2 * DA_HEADS, 1, T_Q), F32),
            pltpu.VMEM((2 * DA_HEADS, V_ROWS, T_Q), F32),
            pltpu.VMEM((2 * DA_HEADS, T_S, T_Q), F32),
            pltpu.VMEM((2 * DA_HEADS, 1, T_Q), F32),
            pltpu.VMEM((2 * DA_HEADS, T_S, T_Q), BF16),
            pltpu.VMEM((2 * DA_HEADS, 1, T_Q), F32),
            pltpu.VMEM((T_Q, DA_INNER), F32),
    ]


def kernel(x, ab_norm_g, ab_w_in, ab_ssd_conv_w, ab_ssd_conv_b, ab_dt_bias, ab_a_log, ab_d_skip, ab_ssd_norm_g,
           ab_ml_conv_w, ab_ml_conv_b, ab_i_bias, ab_f_bias, ab_ml_norm_g, ab_w_out,
           c_norm_g, c_w_in, c_lam_q1, c_lam_k1, c_lam_q2, c_lam_k2, c_subln_g, c_w_out, final_norm_g):
    x1 = _layer_ab(x, ab_norm_g[0], ab_w_in[0], ab_ssd_conv_w[0], ab_ssd_conv_b[0], ab_dt_bias[0], ab_a_log[0],
                   ab_d_skip[0], ab_ssd_norm_g[0], ab_ml_conv_w[0], ab_ml_conv_b[0], ab_i_bias[0], ab_f_bias[0],
                   ab_ml_norm_g[0], ab_w_out[0])
    q, k, vt, z = _layer_c_proj(x1, c_norm_g[0], c_w_in[0])
    return _layer_c_attn(x1, q, k, vt, z, c_lam_q1[0], c_lam_k1[0], c_lam_q2[0], c_lam_k2[0],
                         c_subln_g[0], c_w_out[0], final_norm_g)
```

```python
import math

import ml_dtypes
import numpy as np
import jax
import jax.numpy as jnp
from jax import lax
from jax.experimental import pallas as pl
from jax.experimental.pallas import tpu as pltpu

F32 = jnp.float32
BF16 = jnp.bfloat16

CHUNK = 64
EPS = 1e-6
NEG = -1e30

SSD_HEADS = 16
SSD_HEAD_DIM = 64
SSD_INNER = 1024
SSD_GROUPS = 2
SSD_HPG = 8
SSD_STATE = 128
SSD_GN = 256
SSD_XBC = 1536

ML_HEADS = 4
ML_QK_DIM = 128
ML_V_DIM = 256
ML_QK = 512
ML_INNER = 1024

DA_HEADS = 8
DA_QK_DIM = 64
DA_V_DIM = 128
DA_QK = 1024
DA_INNER = 1024

AB_SIZES = (SSD_INNER, SSD_XBC, SSD_HEADS, ML_INNER, ML_QK, ML_QK, ML_INNER, ML_HEADS, ML_HEADS, ML_INNER)
AB_OFFS = tuple(int(s) for s in np.cumsum((0,) + AB_SIZES))

LANES = 128
SUBLANES = 8

T_AB = 256
SM_IG = 16
SM_FG = 20
VMEM_LIMIT = 56 * 1024 * 1024


def _split3(a):
    hi = a.astype(BF16)
    r = a - hi.astype(F32)
    mid = r.astype(BF16)
    lo = (r - mid.astype(F32)).astype(BF16)
    return hi, mid, lo


def _dot(a, b):
    return jnp.dot(a, b, preferred_element_type=F32)


def _dot_nt(a, b):
    return lax.dot_general(a, b, (((1,), (1,)), ((), ())), preferred_element_type=F32)


def _dot_tn(a, b):
    return lax.dot_general(a, b, (((0,), (0,)), ((), ())), preferred_element_type=F32)


def _sigmoid(v):
    return 1.0 / (1.0 + jnp.exp(-v))


def _silu(v):
    return v * _sigmoid(v)


def _ab_kernel(x_ref, g_ref, wza_ref, wxbc_ref, wzb_ref, wqk_ref, wv_ref, wog_ref, wsm_ref,
               scw_ref, scb_ref, smb_ref, sma_ref, dskip_ref, sng_ref,
               mcw_ref, mcb_ref, mng_ref, wout_ref, tril_ref, e16_ref,
               out_ref,
               xbcp, qkp, sst, cst, nst, mst, ybuf):
    T = T_AB
    c = pl.program_id(1)

    @pl.when(c == 0)
    def _():
        xbcp[0:SUBLANES, :] = jnp.zeros((SUBLANES, SSD_XBC), F32)
        qkp[0:SUBLANES, :] = jnp.zeros((SUBLANES, 2 * ML_QK), F32)
        sst[...] = jnp.zeros_like(sst)
        cst[...] = jnp.zeros_like(cst)
        nst[...] = jnp.zeros_like(nst)
        mst[...] = jnp.zeros_like(mst)

    x = x_ref[0]
    ms = jnp.mean(x * x, axis=-1, keepdims=True)
    h = (x * lax.rsqrt(ms + EPS) * g_ref[...]).astype(BF16)

    def conv(pad_ref, w_ref, cw_ref, cb_ref):
        cur = _dot(h, w_ref[...])
        pad_ref[SUBLANES:SUBLANES + T, :] = cur
        acc = cb_ref[...] + cw_ref[3:4, :] * cur
        for k in range(3):
            off = SUBLANES - 3 + k
            acc = acc + cw_ref[k:k + 1, :] * pad_ref[off:off + T, :]
        pad_ref[0:SUBLANES, :] = pad_ref[T:T + SUBLANES, :]
        return _silu(acc)

    xbc = conv(xbcp, wxbc_ref, scw_ref, scb_ref)
    qk = conv(qkp, wqk_ref, mcw_ref, mcb_ref)

    lane = lax.broadcasted_iota(jnp.int32, (T, LANES), 1)
    sv = _dot(h, wsm_ref[...]) + smb_ref[...]
    e = jnp.log1p(jnp.exp(-jnp.abs(sv)))
    dt = jnp.where(lane < SM_IG, jnp.maximum(sv, 0.0) + e, 0.0)
    fgl = jnp.minimum(sv, 0.0) - e
    cs_in = jnp.where(lane < SM_IG, dt * sma_ref[...],
                      jnp.where(lane < SM_FG, 0.0, jnp.where(lane < SM_FG + ML_HEADS, fgl, 0.0)))
    tril = tril_ref[...]
    c_hi, c_mid, c_lo = _split3(cs_in)
    cs = _dot(tril, c_hi) + _dot(tril, c_mid) + _dot(tril, c_lo)
    scol = jnp.where(lane < SM_IG, cs, jnp.where(lane < SM_FG, sv, cs))
    srow = scol.T
    cs_end = cs[T - 1:T, :]
    eacs = jnp.where(lane < SM_IG, jnp.exp(cs), 0.0)
    toend = jnp.where(lane < SM_IG, jnp.exp(cs_end - cs), 0.0)

    e16 = e16_ref[...]

    def expand(a):
        return _dot(jnp.concatenate(_split3(a), axis=1), e16)

    dt_x = expand(dt)
    eacs_x = expand(eacs)
    toend_x = expand(toend)

    rowi = lax.broadcasted_iota(jnp.int32, (T, T), 0)
    coli = lax.broadcasted_iota(jnp.int32, (T, T), 1)
    causal = rowi >= coli
    lane64 = lane < SSD_HEAD_DIM

    xs = xbc[:, :SSD_INNER]
    xdt = xs * dt_x
    xdt_end = (xdt * toend_x).astype(BF16)
    eend_x = eacs_x[T - 1:T, :]
    za = _dot(h, wza_ref[...])
    gate_a = _silu(za)
    for g in range(SSD_GROUPS):
        bm = xbc[:, SSD_INNER + g * SSD_STATE:SSD_INNER + (g + 1) * SSD_STATE].astype(BF16)
        cm = xbc[:, SSD_INNER + SSD_GN + g * SSD_STATE:SSD_INNER + SSD_GN + (g + 1) * SSD_STATE].astype(BF16)
        cb = _dot_nt(cm, bm)
        gs = slice(g * 512, (g + 1) * 512)
        s_old = sst[g]
        yoff = _dot(cm, s_old.astype(BF16)) * eacs_x[:, gs]
        for p in range(SSD_HPG // 2):
            hd0 = g * SSD_HPG + 2 * p
            ms_ = []
            for hd in (hd0, hd0 + 1):
                seg = scol[:, hd:hd + 1] - srow[hd:hd + 1, :]
                dec = jnp.exp(jnp.where(causal, seg, NEG))
                ms_.append((cb * dec).astype(BF16))
            mcat = jnp.concatenate(ms_, axis=1)
            ls = slice(hd0 * SSD_HEAD_DIM, (hd0 + 2) * SSD_HEAD_DIM)
            slab = xdt[:, ls]
            bd = jnp.concatenate([jnp.where(lane64, slab, 0.0).astype(BF16),
                                  jnp.where(lane64, 0.0, slab).astype(BF16)], axis=0)
            y = _dot(mcat, bd) + yoff[:, 2 * p * SSD_HEAD_DIM:(2 * p + 2) * SSD_HEAD_DIM]
            y = y + xs[:, ls] * dskip_ref[:, ls]
            ybuf[:, ls] = y * gate_a[:, ls]
        sst[g] = s_old * eend_x[:, gs] + _dot_tn(bm, xdt_end[:, gs])
    for g in range(SSD_GROUPS):
        gs = slice(g * 512, (g + 1) * 512)
        yg = ybuf[:, gs]
        inv = lax.rsqrt(jnp.mean(yg * yg, axis=-1, keepdims=True) + EPS)
        ybuf[:, gs] = yg * inv * sng_ref[:, gs]

    vv = _dot(h, wv_ref[...])
    og = _dot(h, wog_ref[...])
    zb = _dot(h, wzb_ref[...])
    for hd in range(ML_HEADS):
        q_h = qk[:, hd * ML_QK_DIM:(hd + 1) * ML_QK_DIM]
        k_h = qk[:, ML_QK + hd * ML_QK_DIM:ML_QK + (hd + 1) * ML_QK_DIM] * (ML_QK_DIM ** -0.5)
        vs = slice(hd * ML_V_DIM, (hd + 1) * ML_V_DIM)
        v_h = vv[:, vs].astype(BF16)
        q_b = q_h.astype(BF16)
        b_c = scol[:, SM_FG + hd:SM_FG + hd + 1]
        b_r = srow[SM_FG + hd:SM_FG + hd + 1, :]
        ig_c = scol[:, SM_IG + hd:SM_IG + hd + 1]
        ig_r = srow[SM_IG + hd:SM_IG + hd + 1, :]
        mp = mst[hd:hd + 1, 0:1]
        c_old = cst[hd]
        n_old = nst[hd:hd + 1, :]
        dlog = jnp.where(causal, b_c - b_r + ig_r, NEG)
        inter = b_c + mp
        m_t = jnp.maximum(inter, jnp.max(dlog, axis=1, keepdims=True))
        w_intra = jnp.exp(dlog - m_t)
        w_inter = jnp.exp(inter - m_t)
        s = _dot_nt(q_b, k_h.astype(BF16)) * w_intra
        num = _dot(s.astype(BF16), v_h) + w_inter * _dot(q_b, c_old.astype(BF16))
        den = jnp.sum(s, axis=1, keepdims=True) + w_inter * jnp.sum(q_h * n_old, axis=1, keepdims=True)
        hh = num / jnp.maximum(jnp.abs(den), jnp.exp(-m_t))
        b_end = b_c[T - 1:T, :]
        glog = b_end - b_c + ig_c
        m_new = jnp.maximum(b_end + mp, jnp.max(glog, axis=0, keepdims=True))
        kg = k_h * jnp.exp(glog - m_new)
        dec = jnp.exp(b_end + mp - m_new)
        cst[hd] = dec * c_old + _dot_tn(kg.astype(BF16), v_h)
        nst[hd:hd + 1, :] = dec * n_old + jnp.sum(kg, axis=0, keepdims=True)
        mst[hd:hd + 1, :] = jnp.broadcast_to(m_new, (1, LANES))
        ho = hh * _sigmoid(og[:, vs])
        inv = lax.rsqrt(jnp.mean(ho * ho, axis=-1, keepdims=True) + EPS)
        ybuf[:, ML_INNER + hd * ML_V_DIM:ML_INNER + (hd + 1) * ML_V_DIM] = (
            ho * inv * mng_ref[:, vs] * _silu(zb[:, vs]))

    y = _dot(ybuf[...].astype(BF16), wout_ref[...])
    out_ref[0] = x + y


def _const_spec(shape):
    nd = len(shape)
    return pl.BlockSpec(shape, lambda b, c: (0,) * nd, pipeline_mode=pl.Buffered(1))


def _layer_ab(x, norm_g, w_in, ssd_cw, ssd_cb, dt_bias, a_log, d_skip, ssd_ng,
              ml_cw, ml_cb, i_bias, f_bias, ml_ng, w_out):
    B, L, D = x.shape
    T = T_AB
    o = AB_OFFS
    seg = lambda i: w_in[:, o[i]:o[i + 1]]
    wza, wxbc, wdt, wzb, wq, wk, wv, wig, wfg, wog = [seg(i) for i in range(10)]
    wsm = jnp.concatenate([wdt, wig, wfg, jnp.zeros((D, LANES - SM_FG - ML_HEADS), F32)], axis=1)
    wqk = jnp.concatenate([wq, wk], axis=1)
    pad = jnp.zeros((LANES - SM_FG - ML_HEADS,), F32)
    smb = jnp.concatenate([dt_bias, i_bias, f_bias, pad])[None, :]
    sma = jnp.concatenate([-jnp.exp(a_log), jnp.zeros((LANES - SSD_HEADS,), F32)])[None, :]
    dskip = jnp.repeat(d_skip, SSD_HEAD_DIM)[None, :]
    tril = jnp.asarray(np.tril(np.ones((T, T), np.float32)), BF16)
    e16_np = np.zeros((LANES, SSD_INNER), np.float32)
    for hd in range(SSD_HEADS):
        e16_np[hd, hd * SSD_HEAD_DIM:(hd + 1) * SSD_HEAD_DIM] = 1.0
    e16 = jnp.asarray(np.concatenate([e16_np] * 3, axis=0), BF16)
    bf = lambda a: a.astype(BF16)
    row = lambda a: a[None, :]
    args = (x, row(norm_g), bf(wza), bf(wxbc), bf(wzb), bf(wqk), bf(wv), bf(wog), bf(wsm),
            ssd_cw, row(ssd_cb), smb, sma, dskip, row(ssd_ng),
            ml_cw, row(ml_cb), row(ml_ng), bf(w_out), tril, e16)
    in_specs = [pl.BlockSpec((1, T, D), lambda b, c: (b, c, 0))]
    in_specs += [_const_spec(a.shape) for a in args[1:]]
    return pl.pallas_call(
        _ab_kernel,
        grid=(B, L // T),
        in_specs=in_specs,
        out_specs=pl.BlockSpec((1, T, D), lambda b, c: (b, c, 0)),
        out_shape=jax.ShapeDtypeStruct((B, L, D), F32),
        scratch_shapes=[
            pltpu.VMEM((T + SUBLANES, SSD_XBC), F32),
            pltpu.VMEM((T + SUBLANES, 2 * ML_QK), F32),
            pltpu.VMEM((SSD_GROUPS, SSD_STATE, SSD_HPG * SSD_HEAD_DIM), F32),
            pltpu.VMEM((ML_HEADS, ML_QK_DIM, ML_V_DIM), F32),
            pltpu.VMEM((SUBLANES, ML_QK_DIM), F32),
            pltpu.VMEM((SUBLANES, LANES), F32),
            pltpu.VMEM((T, SSD_INNER + ML_INNER), F32),
        ],
        compiler_params=pltpu.CompilerParams(
            dimension_semantics=("arbitrary", "arbitrary"), vmem_limit_bytes=VMEM_LIMIT),
        name="layer_ab",
    )(*args)


T_Q = 256
T_S = 256
UNROLL = 4
N_AHEAD = UNROLL + 1
N_BUF = 16
V_ROWS = DA_V_DIM + 16
LAYER_C = 1
LAMBDA_INIT = 0.8 - 0.6 * math.exp(-0.3 * LAYER_C)
ALIBI_SLOPES = tuple(2.0 ** (-8.0 * (h + 1) / DA_HEADS) for h in range(DA_HEADS))
LOG2E = 1.4426950408889634


def _cproj_kernel(x_ref, g_ref, wq_ref, wk_ref, wv_ref, wz_ref, q_ref, k_ref, vt_ref, z_ref):
    x = x_ref[0]
    ms = jnp.mean(x * x, axis=-1, keepdims=True)
    h = (x * lax.rsqrt(ms + EPS) * g_ref[...]).astype(BF16)
    q_ref[0] = (_dot(h, wq_ref[...]) * (DA_QK_DIM ** -0.5 * LOG2E)).astype(BF16)
    z_ref[0] = _dot(h, wz_ref[...]).astype(BF16)
    kk = _dot(h, wk_ref[...])
    vv = _dot(h, wv_ref[...])
    tm = x.shape[0]
    for hd in range(DA_HEADS):
        ls = slice(hd * DA_V_DIM, (hd + 1) * DA_V_DIM)
        k_ref[0, hd] = kk[:, ls].astype(BF16)
        vt_ref[0, hd, 0, 0:DA_V_DIM, :] = vv[:, ls].T.astype(BF16)
        vt_ref[0, hd, 0, DA_V_DIM:V_ROWS, :] = jnp.ones((V_ROWS - DA_V_DIM, tm), BF16)


def _layer_c_proj(x, norm_g, w_in):
    B, L, D = x.shape
    tm = T_S
    wq, wk, wv, wz = [w_in[:, i * DA_QK:(i + 1) * DA_QK].astype(BF16) for i in range(4)]
    cspec = lambda shape: pl.BlockSpec(shape, lambda b, i: (0,) * len(shape), pipeline_mode=pl.Buffered(1))
    return pl.pallas_call(
        _cproj_kernel,
        grid=(B, L // tm),
        in_specs=[pl.BlockSpec((1, tm, D), lambda b, i: (b, i, 0)), cspec((1, D)),
                  cspec((D, DA_QK)), cspec((D, DA_QK)), cspec((D, DA_INNER)), cspec((D, DA_INNER))],
        out_specs=[pl.BlockSpec((1, tm, DA_QK), lambda b, i: (b, i, 0)),
                   pl.BlockSpec((1, DA_HEADS, tm, 2 * DA_QK_DIM), lambda b, i: (b, 0, i, 0)),
                   pl.BlockSpec((1, DA_HEADS, 1, V_ROWS, tm), lambda b, i: (b, 0, i, 0, 0)),
                   pl.BlockSpec((1, tm, DA_INNER), lambda b, i: (b, i, 0))],
        out_shape=[jax.ShapeDtypeStruct((B, L, DA_QK), BF16),
                   jax.ShapeDtypeStruct((B, DA_HEADS, L, 2 * DA_QK_DIM), BF16),
                   jax.ShapeDtypeStruct((B, DA_HEADS, L // tm, V_ROWS, tm), BF16),
                   jax.ShapeDtypeStruct((B, L, DA_INNER), BF16)],
        compiler_params=pltpu.CompilerParams(
            dimension_semantics=("arbitrary", "arbitrary"), vmem_limit_bytes=VMEM_LIMIT),
        name="layer_c_proj",
    )(x, norm_g[None, :], wq, wk, wv, wz)


def _attn_kernel(q_ref, k_hbm, vt_hbm, z_ref, x_ref, kaug_ref, qaug_ref, dbias_ref, cvec_ref, lamv_ref,
                 subg_ref, fng_ref, wout_ref, out_ref, kbuf, vbuf, sem, qt_scr, m_scr, acc_scr, st_scr, cmax_scr,
                 pt_scr, alpha_scr, ybuf):
    b = pl.program_id(0)
    i = pl.program_id(1)
    qk_rows = 2 * DA_QK_DIM

    def kv_copies(u, bb=b):
        slot = u % N_BUF
        rows = pl.ds(pl.multiple_of(u * T_S, T_S), T_S)
        return (pltpu.make_async_copy(k_hbm.at[bb, :, rows, :], kbuf.at[slot], sem.at[0, slot]),
                pltpu.make_async_copy(vt_hbm.at[bb, :, u], vbuf.at[slot], sem.at[1, slot]))

    def fetch(u, bb=b):
        for cp in kv_copies(u, bb):
            cp.start()

    def wait(u):
        for cp in kv_copies(u):
            cp.wait()

    def fetch_head(bb, ii):
        for u in range(N_AHEAD):
            @pl.when(u <= ii)
            def _():
                fetch(u, bb)

    @pl.when((b == 0) & (i == 0))
    def _():
        fetch_head(b, i)

    def init():
        m_scr[...] = jnp.full(m_scr.shape, NEG, F32)
        acc_scr[...] = jnp.zeros_like(acc_scr)
        pt_scr[...] = jnp.zeros_like(pt_scr)
        alpha_scr[...] = jnp.ones_like(alpha_scr)
        rowi = lax.broadcasted_iota(jnp.int32, (qk_rows, T_Q), 0)
        for hd in range(DA_HEADS):
            qht = q_ref[0, :, hd * qk_rows:(hd + 1) * qk_rows].astype(F32).T
            qt_scr[2 * hd, 0:qk_rows, :] = jnp.where(rowi < DA_QK_DIM, qht, 0.0).astype(BF16)
            qt_scr[2 * hd + 1, 0:qk_rows, :] = jnp.where(rowi < DA_QK_DIM, 0.0, qht).astype(BF16)
            qt_scr[2 * hd, qk_rows:, :] = qaug_ref[hd]
            qt_scr[2 * hd + 1, qk_rows:, :] = qaug_ref[hd]

    init()

    n_hh = 2 * DA_HEADS
    kaug = kaug_ref[...]

    def stage_a_full(hh, u):
        lhs = jnp.concatenate([kbuf[u % N_BUF, hh // 2], kaug], axis=1)
        st = _dot(lhs, qt_scr[hh])
        st_scr[hh] = st
        cmax_scr[hh] = jnp.max(st, axis=0, keepdims=True)

    def stage_a_diag(hh, u):
        st = _dot(kbuf[u % N_BUF, hh // 2], qt_scr[hh, 0:qk_rows, :]) + dbias_ref[hh // 2]
        st_scr[hh] = st
        cmax_scr[hh] = jnp.max(st, axis=0, keepdims=True)

    def stage_b(hh, dk):
        coff = cvec_ref[hh // 2] * dk
        m_old = m_scr[hh]
        m_new = jnp.maximum(m_old, cmax_scr[hh] + coff)
        alpha_scr[hh] = jnp.exp2(m_old - m_new)
        pt_scr[hh] = jnp.exp2(st_scr[hh] - (m_new - coff)).astype(BF16)
        m_scr[hh] = m_new

    def stage_c(hh, u):
        acc_scr[hh] = acc_scr[hh] * alpha_scr[hh] + _dot(vbuf[u % N_BUF, hh // 2], pt_scr[hh])

    def key_offset(u):
        return jnp.full((1, T_Q), (u - i) * T_S, jnp.int32).astype(F32)

    def body(u, n_sub, stage_a_last):
        for t in range(n_sub):
            wait(u + 1 + t)
        for t in range(n_sub):
            @pl.when(u + N_AHEAD + t <= i)
            def _():
                fetch(u + N_AHEAD + t)
        for t in range(n_sub):
            v = u + t
            dk = key_offset(v)
            vp = jnp.maximum(v - 1, 0)
            stage_a = stage_a_last if t == n_sub - 1 else stage_a_full
            for hh in range(n_hh):
                stage_c(hh, vp)
                stage_b(hh, dk)
                stage_a(hh, v + 1)

    wait(0)

    @pl.when(i == 0)
    def _():
        for hh in range(n_hh):
            stage_a_diag(hh, 0)

    @pl.when(i > 0)
    def _():
        for hh in range(n_hh):
            stage_a_full(hh, 0)

        def loop_body(p, carry):
            body(UNROLL * p, UNROLL, stage_a_full)
            return carry

        n_loop = (i - 1) // UNROLL
        lax.fori_loop(0, n_loop, loop_body, 0)
        rem = (i - 1) % UNROLL
        done = n_loop * UNROLL
        piece = UNROLL // 2
        while piece >= 1:
            @pl.when(rem & piece != 0)
            def _(done=done, piece=piece):
                body(done, piece, stage_a_full)
            done = done + jnp.where(rem & piece != 0, piece, 0)
            piece //= 2

        body(i - 1, 1, stage_a_diag)

    dk_last = key_offset(i)
    u_prev = jnp.maximum(i - 1, 0)
    for hh in range(n_hh):
        stage_c(hh, u_prev)
        stage_b(hh, dk_last)
    for hh in range(n_hh):
        stage_c(hh, i)

    n_b, n_i = pl.num_programs(0), pl.num_programs(1)
    wrap = i + 1 == n_i
    next_b = jnp.where(wrap, b + 1, b)
    next_i = jnp.where(wrap, 0, i + 1)

    @pl.when(next_b < n_b)
    def _():
        fetch_head(next_b, next_i)

    def epilogue():
        lv = lamv_ref[...]
        lam = (jnp.exp(jnp.sum(lv[0:1] * lv[1:2], axis=1, keepdims=True))
               - jnp.exp(jnp.sum(lv[2:3] * lv[3:4], axis=1, keepdims=True)) + LAMBDA_INIT)
        for hd in range(DA_HEADS):
            a1 = acc_scr[2 * hd]
            a2 = acc_scr[2 * hd + 1]
            o1 = a1[0:DA_V_DIM] / a1[DA_V_DIM:DA_V_DIM + 1]
            o2 = a2[0:DA_V_DIM] / a2[DA_V_DIM:DA_V_DIM + 1]
            ot = o1 - lam * o2
            inv = lax.rsqrt(jnp.mean(ot * ot, axis=0, keepdims=True) + EPS)
            ot = ot * inv * subg_ref[...] * (1.0 - LAMBDA_INIT)
            ls = slice(hd * DA_V_DIM, (hd + 1) * DA_V_DIM)
            ybuf[:, ls] = ot.T * _silu(z_ref[0, :, ls].astype(F32))
        y = _dot(ybuf[...].astype(BF16), wout_ref[...])
        xo = x_ref[0] + y
        inv = lax.rsqrt(jnp.mean(xo * xo, axis=-1, keepdims=True) + EPS)
        out_ref[0] = xo * inv * fng_ref[...]

    epilogue()


def _np_split3(c):
    rb = lambda a: a.astype(ml_dtypes.bfloat16).astype(np.float32)
    hi = rb(c)
    mid = rb(c - hi)
    lo = rb(c - hi - mid)
    return hi, mid, lo


def _attn_tables():
    c32 = (np.asarray(ALIBI_SLOPES, np.float64) * LOG2E).astype(np.float32)
    jj = np.arange(T_S, dtype=np.float32)[:, None]
    ii = np.arange(T_Q, dtype=np.float32)[None, :]
    allowed = (jj // CHUNK) <= (ii // CHUNK)
    kaug = np.zeros((T_S, LANES), np.float32)
    kaug[:, 0:3] = jj
    qaug = np.zeros((DA_HEADS, LANES, T_Q), np.float32)
    for r, part in enumerate(_np_split3(c32)):
        qaug[:, r, :] = part[:, None]
    dbias = np.where(allowed[None], c32[:, None, None] * (ii - np.abs(ii - jj))[None], np.float32(NEG))
    cvec = np.broadcast_to(c32[:, None, None], (DA_HEADS, 1, T_Q))
    return (jnp.asarray(kaug, BF16), jnp.asarray(qaug, BF16), jnp.asarray(dbias, F32), jnp.asarray(cvec, F32))


def _layer_c_attn(x, q, k, vt, z, lam_q1, lam_k1, lam_q2, lam_k2, subln_g, w_out, final_g):
    B, L, D = x.shape
    lamv = jnp.zeros((SUBLANES, LANES), F32)
    for r, v in enumerate((lam_q1, lam_k1, lam_q2, lam_k2)):
        lamv = lamv.at[r, :DA_QK_DIM].set(v)
    kaug, qaug, dbias, cvec = _attn_tables()
    cspec = lambda shape: pl.BlockSpec(shape, lambda b, i: (0,) * len(shape), pipeline_mode=pl.Buffered(1))
    row_spec = pl.BlockSpec((1, T_Q, D), lambda b, i: (b, i, 0))
    hbm_spec = pl.BlockSpec(memory_space=pl.ANY)
    in_specs = [row_spec, hbm_spec, hbm_spec, row_spec, row_spec,
                cspec((T_S, LANES)), cspec((DA_HEADS, LANES, T_Q)), cspec((DA_HEADS, T_S, T_Q)),
                cspec((DA_HEADS, 1, T_Q)), cspec((SUBLANES, LANES)), cspec((DA_V_DIM, 1)),
                cspec((1, D)), cspec((DA_INNER, D))]
    return pl.pallas_call(
        _attn_kernel,
        grid=(B, L // T_Q),
        in_specs=in_specs,
        out_specs=row_spec,
        scratch_shapes=_attn_scratch(),
        out_shape=jax.ShapeDtypeStruct((B, L, D), F32),
        compiler_params=pltpu.CompilerParams(
            dimension_semantics=("arbitrary", "arbitrary"), vmem_limit_bytes=VMEM_LIMIT),
        name="layer_c_attn",
    )(q, k, vt, z, x, kaug, qaug, dbias, cvec, lamv, subln_g[:, None], final_g[None, :], w_out.astype(BF16))


def _attn_scratch():
    n_hh = 2 * DA_HEADS
    return [
        pltpu.VMEM((N_BUF, DA_HEADS, T_S, 2 * DA_QK_DIM), BF16),
        pltpu.VMEM((N_BUF, DA_HEADS, V_ROWS, T_S), BF16),
        pltpu.SemaphoreType.DMA((2, N_BUF)),
        pltpu.VMEM((n_hh, 2 * DA_QK_DIM + LANES, T_Q), BF16),
        pltpu.VMEM((n_hh, 1, T_Q), F32),
        pltpu.VMEM((n_hh, V_ROWS, T_Q), F32),
        pltpu.VMEM((n_hh, T_S, T_Q), F32),
        pltpu.VMEM((n_hh, 1, T_Q), F32),
        pltpu.VMEM((n_hh, T_S, T_Q), BF16),
        pltpu.VMEM((n_hh, 1, T_Q), F32),
        pltpu.VMEM((T_Q, DA_INNER), F32),
    ]


def kernel(x, ab_norm_g, ab_w_in, ab_ssd_conv_w, ab_ssd_conv_b, ab_dt_bias, ab_a_log, ab_d_skip, ab_ssd_norm_g,
           ab_ml_conv_w, ab_ml_conv_b, ab_i_bias, ab_f_bias, ab_ml_norm_g, ab_w_out,
           c_norm_g, c_w_in, c_lam_q1, c_lam_k1, c_lam_q2, c_lam_k2, c_subln_g, c_w_out, final_norm_g):
    x1 = _layer_ab(x, ab_norm_g[0], ab_w_in[0], ab_ssd_conv_w[0], ab_ssd_conv_b[0], ab_dt_bias[0], ab_a_log[0],
                   ab_d_skip[0], ab_ssd_norm_g[0], ab_ml_conv_w[0], ab_ml_conv_b[0], ab_i_bias[0], ab_f_bias[0],
                   ab_ml_norm_g[0], ab_w_out[0])
    q, k, vt, z = _layer_c_proj(x1, c_norm_g[0], c_w_in[0])
    return _layer_c_attn(x1, q, k, vt, z, c_lam_q1[0], c_lam_k1[0], c_lam_q2[0], c_lam_k2[0],
                         c_subln_g[0], c_w_out[0], final_norm_g)
```

```python
import math

import ml_dtypes
import numpy as np
import jax
import jax.numpy as jnp
from jax import lax
from jax.experimental import pallas as pl
from jax.experimental.pallas import tpu as pltpu

F32 = jnp.float32
BF16 = jnp.bfloat16

CHUNK = 64
EPS = 1e-6
NEG = -1e30

SSD_HEADS = 16
SSD_HEAD_DIM = 64
SSD_INNER = 1024
SSD_GROUPS = 2
SSD_HPG = 8
SSD_STATE = 128
SSD_GN = 256
SSD_XBC = 1536

ML_HEADS = 4
ML_QK_DIM = 128
ML_V_DIM = 256
ML_QK = 512
ML_INNER = 1024

DA_HEADS = 8
DA_QK_DIM = 64
DA_V_DIM = 128
DA_QK = 1024
DA_INNER = 1024

AB_SIZES = (SSD_INNER, SSD_XBC, SSD_HEADS, ML_INNER, ML_QK, ML_QK, ML_INNER, ML_HEADS, ML_HEADS, ML_INNER)
AB_OFFS = tuple(int(s) for s in np.cumsum((0,) + AB_SIZES))

LANES = 128
SUBLANES = 8

T_AB = 256
SM_IG = 16
SM_FG = 20
VMEM_LIMIT = 56 * 1024 * 1024


def _split3(a):
    hi = a.astype(BF16)
    r = a - hi.astype(F32)
    mid = r.astype(BF16)
    lo = (r - mid.astype(F32)).astype(BF16)
    return hi, mid, lo


def _dot(a, b):
    return jnp.dot(a, b, preferred_element_type=F32)


def _dot_nt(a, b):
    return lax.dot_general(a, b, (((1,), (1,)), ((), ())), preferred_element_type=F32)


def _dot_tn(a, b):
    return lax.dot_general(a, b, (((0,), (0,)), ((), ())), preferred_element_type=F32)


def _sigmoid(v):
    return 1.0 / (1.0 + jnp.exp(-v))


def _silu(v):
    return v * _sigmoid(v)


def _ab_kernel(x_ref, g_ref, wza_ref, wxbc_ref, wzb_ref, wqk_ref, wv_ref, wog_ref, wsm_ref,
               scw_ref, scb_ref, smb_ref, sma_ref, dskip_ref, sng_ref,
               mcw_ref, mcb_ref, mng_ref, wout_ref, tril_ref, e16_ref,
               out_ref,
               xbcp, qkp, sst, cst, nst, mst, ybuf):
    T = T_AB
    c = pl.program_id(1)

    @pl.when(c == 0)
    def _():
        xbcp[0:SUBLANES, :] = jnp.zeros((SUBLANES, SSD_XBC), F32)
        qkp[0:SUBLANES, :] = jnp.zeros((SUBLANES, 2 * ML_QK), F32)
        sst[...] = jnp.zeros_like(sst)
        cst[...] = jnp.zeros_like(cst)
        nst[...] = jnp.zeros_like(nst)
        mst[...] = jnp.zeros_like(mst)

    x = x_ref[0]
    ms = jnp.mean(x * x, axis=-1, keepdims=True)
    h = (x * lax.rsqrt(ms + EPS) * g_ref[...]).astype(BF16)

    def conv(pad_ref, w_ref, cw_ref, cb_ref):
        cur = _dot(h, w_ref[...])
        pad_ref[SUBLANES:SUBLANES + T, :] = cur
        acc = cb_ref[...] + cw_ref[3:4, :] * cur
        for k in range(3):
            off = SUBLANES - 3 + k
            acc = acc + cw_ref[k:k + 1, :] * pad_ref[off:off + T, :]
        pad_ref[0:SUBLANES, :] = pad_ref[T:T + SUBLANES, :]
        return _silu(acc)

    xbc = conv(xbcp, wxbc_ref, scw_ref, scb_ref)
    qk = conv(qkp, wqk_ref, mcw_ref, mcb_ref)

    lane = lax.broadcasted_iota(jnp.int32, (T, LANES), 1)
    sv = _dot(h, wsm_ref[...]) + smb_ref[...]
    e = jnp.log1p(jnp.exp(-jnp.abs(sv)))
    dt = jnp.where(lane < SM_IG, jnp.maximum(sv, 0.0) + e, 0.0)
    fgl = jnp.minimum(sv, 0.0) - e
    cs_in = jnp.where(lane < SM_IG, dt * sma_ref[...],
                      jnp.where(lane < SM_FG, 0.0, jnp.where(lane < SM_FG + ML_HEADS, fgl, 0.0)))
    tril = tril_ref[...]
    c_hi, c_mid, c_lo = _split3(cs_in)
    cs = _dot(tril, c_hi) + _dot(tril, c_mid) + _dot(tril, c_lo)
    scol = jnp.where(lane < SM_IG, cs, jnp.where(lane < SM_FG, sv, cs))
    srow = scol.T
    cs_end = cs[T - 1:T, :]
    eacs = jnp.where(lane < SM_IG, jnp.exp(cs), 0.0)
    toend = jnp.where(lane < SM_IG, jnp.exp(cs_end - cs), 0.0)

    e16 = e16_ref[...]

    def expand(a):
        return _dot(jnp.concatenate(_split3(a), axis=1), e16)

    dt_x = expand(dt)
    eacs_x = expand(eacs)
    toend_x = expand(toend)

    rowi = lax.broadcasted_iota(jnp.int32, (T, T), 0)
    coli = lax.broadcasted_iota(jnp.int32, (T, T), 1)
    causal = rowi >= coli
    lane64 = lane < SSD_HEAD_DIM

    xs = xbc[:, :SSD_INNER]
    xdt = xs * dt_x
    xdt_end = (xdt * toend_x).astype(BF16)
    eend_x = eacs_x[T - 1:T, :]
    za = _dot(h, wza_ref[...])
    gate_a = _silu(za)
    for g in range(SSD_GROUPS):
        bm = xbc[:, SSD_INNER + g * SSD_STATE:SSD_INNER + (g + 1) * SSD_STATE].astype(BF16)
        cm = xbc[:, SSD_INNER + SSD_GN + g * SSD_STATE:SSD_INNER + SSD_GN + (g + 1) * SSD_STATE].astype(BF16)
        cb = _dot_nt(cm, bm)
        gs = slice(g * 512, (g + 1) * 512)
        s_old = sst[g]
        yoff = _dot(cm, s_old.astype(BF16)) * eacs_x[:, gs]
        for p in range(SSD_HPG // 2):
            hd0 = g * SSD_HPG + 2 * p
            ms_ = []
            for hd in (hd0, hd0 + 1):
                seg = scol[:, hd:hd + 1] - srow[hd:hd + 1, :]
                dec = jnp.exp(jnp.where(causal, seg, NEG))
                ms_.append((cb * dec).astype(BF16))
            mcat = jnp.concatenate(ms_, axis=1)
            ls = slice(hd0 * SSD_HEAD_DIM, (hd0 + 2) * SSD_HEAD_DIM)
            slab = xdt[:, ls]
            bd = jnp.concatenate([jnp.where(lane64, slab, 0.0).astype(BF16),
                                  jnp.where(lane64, 0.0, slab).astype(BF16)], axis=0)
            y = _dot(mcat, bd) + yoff[:, 2 * p * SSD_HEAD_DIM:(2 * p + 2) * SSD_HEAD_DIM]
            y = y + xs[:, ls] * dskip_ref[:, ls]
            ybuf[:, ls] = y * gate_a[:, ls]
        sst[g] = s_old * eend_x[:, gs] + _dot_tn(bm, xdt_end[:, gs])
    for g in range(SSD_GROUPS):
        gs = slice(g * 512, (g + 1) * 512)
        yg = ybuf[:, gs]
        inv = lax.rsqrt(jnp.mean(yg * yg, axis=-1, keepdims=True) + EPS)
        ybuf[:, gs] = yg * inv * sng_ref[:, gs]

    vv = _dot(h, wv_ref[...])
    og = _dot(h, wog_ref[...])
    zb = _dot(h, wzb_ref[...])
    for hd in range(ML_HEADS):
        q_h = qk[:, hd * ML_QK_DIM:(hd + 1) * ML_QK_DIM]
        k_h = qk[:, ML_QK + hd * ML_QK_DIM:ML_QK + (hd + 1) * ML_QK_DIM] * (ML_QK_DIM ** -0.5)
        vs = slice(hd * ML_V_DIM, (hd + 1) * ML_V_DIM)
        v_h = vv[:, vs].astype(BF16)
        q_b = q_h.astype(BF16)
        b_c = scol[:, SM_FG + hd:SM_FG + hd + 1]
        b_r = srow[SM_FG + hd:SM_FG + hd + 1, :]
        ig_c = scol[:, SM_IG + hd:SM_IG + hd + 1]
        ig_r = srow[SM_IG + hd:SM_IG + hd + 1, :]
        mp = mst[hd:hd + 1, 0:1]
        c_old = cst[hd]
        n_old = nst[hd:hd + 1, :]
        dlog = jnp.where(causal, b_c - b_r + ig_r, NEG)
        inter = b_c + mp
        m_t = jnp.maximum(inter, jnp.max(dlog, axis=1, keepdims=True))
        w_intra = jnp.exp(dlog - m_t)
        w_inter = jnp.exp(inter - m_t)
        s = _dot_nt(q_b, k_h.astype(BF16)) * w_intra
        num = _dot(s.astype(BF16), v_h) + w_inter * _dot(q_b, c_old.astype(BF16))
        den = jnp.sum(s, axis=1, keepdims=True) + w_inter * jnp.sum(q_h * n_old, axis=1, keepdims=True)
        hh = num / jnp.maximum(jnp.abs(den), jnp.exp(-m_t))
        b_end = b_c[T - 1:T, :]
        glog = b_end - b_c + ig_c
        m_new = jnp.maximum(b_end + mp, jnp.max(glog, axis=0, keepdims=True))
        kg = k_h * jnp.exp(glog - m_new)
        dec = jnp.exp(b_end + mp - m_new)
        cst[hd] = dec * c_old + _dot_tn(kg.astype(BF16), v_h)
        nst[hd:hd + 1, :] = dec * n_old + jnp.sum(kg, axis=0, keepdims=True)
        mst[hd:hd + 1, :] = jnp.broadcast_to(m_new, (1, LANES))
        ho = hh * _sigmoid(og[:, vs])
        inv = lax.rsqrt(jnp.mean(ho * ho, axis=-1, keepdims=True) + EPS)
        ybuf[:, ML_INNER + hd * ML_V_DIM:ML_INNER + (hd + 1) * ML_V_DIM] = (
            ho * inv * mng_ref[:, vs] * _silu(zb[:, vs]))

    y = _dot(ybuf[...].astype(BF16), wout_ref[...])
    out_ref[0] = x + y


def _const_spec(shape):
    nd = len(shape)
    return pl.BlockSpec(shape, lambda b, c: (0,) * nd, pipeline_mode=pl.Buffered(1))


def _layer_ab(x, norm_g, w_in, ssd_cw, ssd_cb, dt_bias, a_log, d_skip, ssd_ng,
              ml_cw, ml_cb, i_bias, f_bias, ml_ng, w_out):
    B, L, D = x.shape
    T = T_AB
    o = AB_OFFS
    seg = lambda i: w_in[:, o[i]:o[i + 1]]
    wza, wxbc, wdt, wzb, wq, wk, wv, wig, wfg, wog = [seg(i) for i in range(10)]
    wsm = jnp.concatenate([wdt, wig, wfg, jnp.zeros((D, LANES - SM_FG - ML_HEADS), F32)], axis=1)
    wqk = jnp.concatenate([wq, wk], axis=1)
    pad = jnp.zeros((LANES - SM_FG - ML_HEADS,), F32)
    smb = jnp.concatenate([dt_bias, i_bias, f_bias, pad])[None, :]
    sma = jnp.concatenate([-jnp.exp(a_log), jnp.zeros((LANES - SSD_HEADS,), F32)])[None, :]
    dskip = jnp.repeat(d_skip, SSD_HEAD_DIM)[None, :]
    tril = jnp.asarray(np.tril(np.ones((T, T), np.float32)), BF16)
    e16_np = np.zeros((LANES, SSD_INNER), np.float32)
    for hd in range(SSD_HEADS):
        e16_np[hd, hd * SSD_HEAD_DIM:(hd + 1) * SSD_HEAD_DIM] = 1.0
    e16 = jnp.asarray(np.concatenate([e16_np] * 3, axis=0), BF16)
    bf = lambda a: a.astype(BF16)
    row = lambda a: a[None, :]
    args = (x, row(norm_g), bf(wza), bf(wxbc), bf(wzb), bf(wqk), bf(wv), bf(wog), bf(wsm),
            ssd_cw, row(ssd_cb), smb, sma, dskip, row(ssd_ng),
            ml_cw, row(ml_cb), row(ml_ng), bf(w_out), tril, e16)
    in_specs = [pl.BlockSpec((1, T, D), lambda b, c: (b, c, 0))]
    in_specs += [_const_spec(a.shape) for a in args[1:]]
    return pl.pallas_call(
        _ab_kernel,
        grid=(B, L // T),
        in_specs=in_specs,
        out_specs=pl.BlockSpec((1, T, D), lambda b, c: (b, c, 0)),
        out_shape=jax.ShapeDtypeStruct((B, L, D), F32),
        scratch_shapes=[
            pltpu.VMEM((T + SUBLANES, SSD_XBC), F32),
            pltpu.VMEM((T + SUBLANES, 2 * ML_QK), F32),
            pltpu.VMEM((SSD_GROUPS, SSD_STATE, SSD_HPG * SSD_HEAD_DIM), F32),
            pltpu.VMEM((ML_HEADS, ML_QK_DIM, ML_V_DIM), F32),
            pltpu.VMEM((SUBLANES, ML_QK_DIM), F32),
            pltpu.VMEM((SUBLANES, LANES), F32),
            pltpu.VMEM((T, SSD_INNER + ML_INNER), F32),
        ],
        compiler_params=pltpu.CompilerParams(
            dimension_semantics=("arbitrary", "arbitrary"), vmem_limit_bytes=VMEM_LIMIT),
        name="layer_ab",
    )(*args)


T_Q = 256
T_S = 256
UNROLL = 4
N_AHEAD = UNROLL + 1
N_BUF = 16
V_ROWS = DA_V_DIM + 16
LAYER_C = 1
LAMBDA_INIT = 0.8 - 0.6 * math.exp(-0.3 * LAYER_C)
ALIBI_SLOPES = tuple(2.0 ** (-8.0 * (h + 1) / DA_HEADS) for h in range(DA_HEADS))
LOG2E = 1.4426950408889634


def _cproj_kernel(x_ref, g_ref, wq_ref, wk_ref, wv_ref, wz_ref, q_ref, k_ref, vt_ref, z_ref):
    x = x_ref[0]
    ms = jnp.mean(x * x, axis=-1, keepdims=True)
    h = (x * lax.rsqrt(ms + EPS) * g_ref[...]).astype(BF16)
    q_ref[0] = (_dot(h, wq_ref[...]) * (DA_QK_DIM ** -0.5 * LOG2E)).astype(BF16)
    z_ref[0] = _dot(h, wz_ref[...]).astype(BF16)
    kk = _dot(h, wk_ref[...])
    vv = _dot(h, wv_ref[...])
    tm = x.shape[0]
    for hd in range(DA_HEADS):
        ls = slice(hd * DA_V_DIM, (hd + 1) * DA_V_DIM)
        k_ref[0, hd] = kk[:, ls].astype(BF16)
        vt_ref[0, hd, 0, 0:DA_V_DIM, :] = vv[:, ls].T.astype(BF16)
        vt_ref[0, hd, 0, DA_V_DIM:V_ROWS, :] = jnp.ones((V_ROWS - DA_V_DIM, tm), BF16)


def _layer_c_proj(x, norm_g, w_in):
    B, L, D = x.shape
    tm = T_S
    wq, wk, wv, wz = [w_in[:, i * DA_QK:(i + 1) * DA_QK].astype(BF16) for i in range(4)]
    cspec = lambda shape: pl.BlockSpec(shape, lambda b, i: (0,) * len(shape), pipeline_mode=pl.Buffered(1))
    return pl.pallas_call(
        _cproj_kernel,
        grid=(B, L // tm),
        in_specs=[pl.BlockSpec((1, tm, D), lambda b, i: (b, i, 0)), cspec((1, D)),
                  cspec((D, DA_QK)), cspec((D, DA_QK)), cspec((D, DA_INNER)), cspec((D, DA_INNER))],
        out_specs=[pl.BlockSpec((1, tm, DA_QK), lambda b, i: (b, i, 0)),
                   pl.BlockSpec((1, DA_HEADS, tm, 2 * DA_QK_DIM), lambda b, i: (b, 0, i, 0)),
                   pl.BlockSpec((1, DA_HEADS, 1, V_ROWS, tm), lambda b, i: (b, 0, i, 0, 0)),
                   pl.BlockSpec((1, tm, DA_INNER), lambda b, i: (b, i, 0))],
        out_shape=[jax.ShapeDtypeStruct((B, L, DA_QK), BF16),
                   jax.ShapeDtypeStruct((B, DA_HEADS, L, 2 * DA_QK_DIM), BF16),
                   jax.ShapeDtypeStruct((B, DA_HEADS, L // tm, V_ROWS, tm), BF16),
                   jax.ShapeDtypeStruct((B, L, DA_INNER), BF16)],
        compiler_params=pltpu.CompilerParams(
            dimension_semantics=("arbitrary", "arbitrary"), vmem_limit_bytes=VMEM_LIMIT),
        name="layer_c_proj",
    )(x, norm_g[None, :], wq, wk, wv, wz)


def _attn_kernel(q_ref, k_hbm, vt_hbm, z_ref, x_ref, kaug_ref, qaug_ref, dbias_ref, cvec_ref, lamv_ref,
                 subg_ref, fng_ref, wout_ref, out_ref, kbuf, vbuf, sem, qt_scr, m_scr, acc_scr, st_scr, cmax_scr,
                 pt_scr, alpha_scr, ybuf):
    b = pl.program_id(0)
    i = pl.program_id(1)
    qk_rows = 2 * DA_QK_DIM

    def kv_copies(u, bb=b):
        slot = u % N_BUF
        rows = pl.ds(pl.multiple_of(u * T_S, T_S), T_S)
        return (pltpu.make_async_copy(k_hbm.at[bb, :, rows, :], kbuf.at[slot], sem.at[0, slot]),
                pltpu.make_async_copy(vt_hbm.at[bb, :, u], vbuf.at[slot], sem.at[1, slot]))

    def fetch(u, bb=b):
        for cp in kv_copies(u, bb):
            cp.start()

    def wait(u):
        for cp in kv_copies(u):
            cp.wait()

    def fetch_head(bb, ii):
        for u in range(N_AHEAD):
            @pl.when(u <= ii)
            def _():
                fetch(u, bb)

    @pl.when((b == 0) & (i == 0))
    def _():
        fetch_head(b, i)

    def init():
        m_scr[...] = jnp.full(m_scr.shape, NEG, F32)
        acc_scr[...] = jnp.zeros_like(acc_scr)
        pt_scr[...] = jnp.zeros_like(pt_scr)
        alpha_scr[...] = jnp.ones_like(alpha_scr)
        rowi = lax.broadcasted_iota(jnp.int32, (qk_rows, T_Q), 0)
        for hd in range(DA_HEADS):
            qht = q_ref[0, :, hd * qk_rows:(hd + 1) * qk_rows].astype(F32).T
            qt_scr[2 * hd, 0:qk_rows, :] = jnp.where(rowi < DA_QK_DIM, qht, 0.0).astype(BF16)
            qt_scr[2 * hd + 1, 0:qk_rows, :] = jnp.where(rowi < DA_QK_DIM, 0.0, qht).astype(BF16)
            qt_scr[2 * hd, qk_rows:, :] = qaug_ref[hd]
            qt_scr[2 * hd + 1, qk_rows:, :] = qaug_ref[hd]

    init()

    n_hh = 2 * DA_HEADS
    kaug = kaug_ref[...]

    def stage_a_full(hh, u):
        lhs = jnp.concatenate([kbuf[u % N_BUF, hh // 2], kaug], axis=1)
        st = _dot(lhs, qt_scr[hh])
        st_scr[hh] = st
        cmax_scr[hh] = jnp.max(st, axis=0, keepdims=True)

    def stage_a_diag(hh, u):
        st = _dot(kbuf[u % N_BUF, hh // 2], qt_scr[hh, 0:qk_rows, :]) + dbias_ref[hh // 2]
        st_scr[hh] = st
        cmax_scr[hh] = jnp.max(st, axis=0, keepdims=True)

    def stage_b(hh, dk):
        coff = cvec_ref[hh // 2] * dk
        m_old = m_scr[hh]
        m_new = jnp.maximum(m_old, cmax_scr[hh] + coff)
        alpha_scr[hh] = jnp.exp2(m_old - m_new)
        pt_scr[hh] = jnp.exp2(st_scr[hh] - (m_new - coff)).astype(BF16)
        m_scr[hh] = m_new

    def stage_c(hh, u):
        acc_scr[hh] = acc_scr[hh] * alpha_scr[hh] + _dot(vbuf[u % N_BUF, hh // 2], pt_scr[hh])

    def key_offset(u):
        return jnp.full((1, T_Q), (u - i) * T_S, jnp.int32).astype(F32)

    def body(u, n_sub, stage_a_last):
        for t in range(n_sub):
            wait(u + 1 + t)
        for t in range(n_sub):
            @pl.when(u + N_AHEAD + t <= i)
            def _():
                fetch(u + N_AHEAD + t)
        for t in range(n_sub):
            v = u + t
            dk = key_offset(v)
            vp = jnp.maximum(v - 1, 0)
            stage_a = stage_a_last if t == n_sub - 1 else stage_a_full
            for hh in range(n_hh):
                stage_c(hh, vp)
                stage_b(hh, dk)
                stage_a(hh, v + 1)

    wait(0)

    @pl.when(i == 0)
    def _():
        for hh in range(n_hh):
            stage_a_diag(hh, 0)

    @pl.when(i > 0)
    def _():
        for hh in range(n_hh):
            stage_a_full(hh, 0)

        def loop_body(p, carry):
            body(UNROLL * p, UNROLL, stage_a_full)
            return carry

        n_loop = (i - 1) // UNROLL
        lax.fori_loop(0, n_loop, loop_body, 0)
        rem = (i - 1) % UNROLL
        done = n_loop * UNROLL
        piece = UNROLL // 2
        while piece >= 1:
            @pl.when(rem & piece != 0)
            def _(done=done, piece=piece):
                body(done, piece, stage_a_full)
            done = done + jnp.where(rem & piece != 0, piece, 0)
            piece //= 2

        body(i - 1, 1, stage_a_diag)

    dk_last = key_offset(i)
    u_prev = jnp.maximum(i - 1, 0)
    for hh in range(n_hh):
        stage_c(hh, u_prev)
        stage_b(hh, dk_last)
    for hh in range(n_hh):
        stage_c(hh, i)

    n_b, n_i = pl.num_programs(0), pl.num_programs(1)
    wrap = i + 1 == n_i
    next_b = jnp.where(wrap, b + 1, b)
    next_i = jnp.where(wrap, 0, i + 1)

    @pl.when(next_b < n_b)
    def _():
        fetch_head(next_b, next_i)

    def epilogue():
        lv = lamv_ref[...]
        lam = (jnp.exp(jnp.sum(lv[0:1] * lv[1:2], axis=1, keepdims=True))
               - jnp.exp(jnp.sum(lv[2:3] * lv[3:4], axis=1, keepdims=True)) + LAMBDA_INIT)
        for hd in range(DA_HEADS):
            a1 = acc_scr[2 * hd]
            a2 = acc_scr[2 * hd + 1]
            o1 = a1[0:DA_V_DIM] / a1[DA_V_DIM:DA_V_DIM + 1]
            o2 = a2[0:DA_V_DIM] / a2[DA_V_DIM:DA_V_DIM + 1]
            ot = o1 - lam * o2
            inv = lax.rsqrt(jnp.mean(ot * ot, axis=0, keepdims=True) + EPS)
            ot = ot * inv * subg_ref[...] * (1.0 - LAMBDA_INIT)
            ls = slice(hd * DA_V_DIM, (hd + 1) * DA_V_DIM)
            ybuf[:, ls] = ot.T * _silu(z_ref[0, :, ls].astype(F32))
        y = _dot(ybuf[...].astype(BF16), wout_ref[...])
        xo = x_ref[0] + y
        inv = lax.rsqrt(jnp.mean(xo * xo, axis=-1, keepdims=True) + EPS)
        out_ref[0] = xo * inv * fng_ref[...]

    epilogue()


def _np_split3(c):
    rb = lambda a: a.astype(ml_dtypes.bfloat16).astype(np.float32)
    hi = rb(c)
    mid = rb(c - hi)
    lo = rb(c - hi - mid)
    return hi, mid, lo


def _attn_tables():
    c32 = (np.asarray(ALIBI_SLOPES, np.float64) * LOG2E).astype(np.float32)
    jj = np.arange(T_S, dtype=np.float32)[:, None]
    ii = np.arange(T_Q, dtype=np.float32)[None, :]
    allowed = (jj // CHUNK) <= (ii // CHUNK)
    kaug = np.zeros((T_S, LANES), np.float32)
    kaug[:, 0:3] = jj
    qaug = np.zeros((DA_HEADS, LANES, T_Q), np.float32)
    for r, part in enumerate(_np_split3(c32)):
        qaug[:, r, :] = part[:, None]
    dbias = np.where(allowed[None], c32[:, None, None] * (ii - np.abs(ii - jj))[None], np.float32(NEG))
    cvec = np.broadcast_to(c32[:, None, None], (DA_HEADS, 1, T_Q))
    return (jnp.asarray(kaug, BF16), jnp.asarray(qaug, BF16), jnp.asarray(dbias, F32), jnp.asarray(cvec, F32))


def _layer_c_attn(x, q, k, vt, z, lam_q1, lam_k1, lam_q2, lam_k2, subln_g, w_out, final_g):
    B, L, D = x.shape
    lamv = jnp.zeros((SUBLANES, LANES), F32)
    for r, v in enumerate((lam_q1, lam_k1, lam_q2, lam_k2)):
        lamv = lamv.at[r, :DA_QK_DIM].set(v)
    kaug, qaug, dbias, cvec = _attn_tables()
    cspec = lambda shape: pl.BlockSpec(shape, lambda b, i: (0,) * len(shape), pipeline_mode=pl.Buffered(1))
    row_spec = pl.BlockSpec((1, T_Q, D), lambda b, i: (b, i, 0))
    hbm_spec = pl.BlockSpec(memory_space=pl.ANY)
    in_specs = [row_spec, hbm_spec, hbm_spec, row_spec, row_spec,
                cspec((T_S, LANES)), cspec((DA_HEADS, LANES, T_Q)), cspec((DA_HEADS, T_S, T_Q)),
                cspec((DA_HEADS, 1, T_Q)), cspec((SUBLANES, LANES)), cspec((DA_V_DIM, 1)),
                cspec((1, D)), cspec((DA_INNER, D))]
    return pl.pallas_call(
        _attn_kernel,
        grid=(B, L // T_Q),
        in_specs=in_specs,
        out_specs=row_spec,
        scratch_shapes=_attn_scratch(),
        out_shape=jax.ShapeDtypeStruct((B, L, D), F32),
        compiler_params=pltpu.CompilerParams(
            dimension_semantics=("arbitrary", "arbitrary"), vmem_limit_bytes=VMEM_LIMIT),
        name="layer_c_attn",
    )(q, k, vt, z, x, kaug, qaug, dbias, cvec, lamv, subln_g[:, None], final_g[None, :], w_out.astype(BF16))


def _attn_scratch():
    n_hh = 2 * DA_HEADS
    return [
        pltpu.VMEM((N_BUF, DA_HEADS, T_S, 2 * DA_QK_DIM), BF16),
        pltpu.VMEM((N_BUF, DA_HEADS, V_ROWS, T_S), BF16),
        pltpu.SemaphoreType.DMA((2, N_BUF)),
        pltpu.VMEM((n_hh, 2 * DA_QK_DIM + LANES, T_Q), BF16),
        pltpu.VMEM((n_hh, 1, T_Q), F32),
        pltpu.VMEM((n_hh, V_ROWS, T_Q), F32),
        pltpu.VMEM((n_hh, T_S, T_Q), F32),
        pltpu.VMEM((n_hh, 1, T_Q), F32),
        pltpu.VMEM((n_hh, T_S, T_Q), BF16),
        pltpu.VMEM((n_hh, 1, T_Q), F32),
        pltpu.VMEM((T_Q, DA_INNER), F32),
    ]


def kernel(x, ab_norm_g, ab_w_in, ab_ssd_conv_w, ab_ssd_conv_b, ab_dt_bias, ab_a_log, ab_d_skip, ab_ssd_norm_g,
           ab_ml_conv_w, ab_ml_conv_b, ab_i_bias, ab_f_bias, ab_ml_norm_g, ab_w_out,
           c_norm_g, c_w_in, c_lam_q1, c_lam_k1, c_lam_q2, c_lam_k2, c_subln_g, c_w_out, final_norm_g):
    assert x.ndim == 3 and x.shape[-1] == DA_INNER and x.dtype == F32, (x.shape, x.dtype)
    assert x.shape[1] % T_AB == 0 and x.shape[1] % T_Q == 0 and T_Q == T_S, x.shape
    assert ab_w_in.shape == (1, x.shape[-1], AB_OFFS[-1]) and c_w_in.shape == (1, x.shape[-1], 4 * DA_QK)
    x1 = _layer_ab(x, ab_norm_g[0], ab_w_in[0], ab_ssd_conv_w[0], ab_ssd_conv_b[0], ab_dt_bias[0], ab_a_log[0],
                   ab_d_skip[0], ab_ssd_norm_g[0], ab_ml_conv_w[0], ab_ml_conv_b[0], ab_i_bias[0], ab_f_bias[0],
                   ab_ml_norm_g[0], ab_w_out[0])
    q, k, vt, z = _layer_c_proj(x1, c_norm_g[0], c_w_in[0])
    return _layer_c_attn(x1, q, k, vt, z, c_lam_q1[0], c_lam_k1[0], c_lam_q2[0], c_lam_k2[0],
                         c_subln_g[0], c_w_out[0], final_norm_g)
```

```python
import math

import ml_dtypes
import numpy as np
import jax
import jax.numpy as jnp
from jax import lax
from jax.experimental import pallas as pl
from jax.experimental.pallas import tpu as pltpu

F32 = jnp.float32
BF16 = jnp.bfloat16

CHUNK = 64
EPS = 1e-6
NEG = -1e30

SSD_HEADS = 16
SSD_HEAD_DIM = 64
SSD_INNER = 1024
SSD_GROUPS = 2
SSD_HPG = 8
SSD_STATE = 128
SSD_GN = 256
SSD_XBC = 1536

ML_HEADS = 4
ML_QK_DIM = 128
ML_V_DIM = 256
ML_QK = 512
ML_INNER = 1024

DA_HEADS = 8
DA_QK_DIM = 64
DA_V_DIM = 128
DA_QK = 1024
DA_INNER = 1024

AB_SIZES = (SSD_INNER, SSD_XBC, SSD_HEADS, ML_INNER, ML_QK, ML_QK, ML_INNER, ML_HEADS, ML_HEADS, ML_INNER)
AB_OFFS = tuple(int(s) for s in np.cumsum((0,) + AB_SIZES))

LANES = 128
SUBLANES = 8

T_AB = 256
SM_IG = 16
SM_FG = 20
VMEM_LIMIT = 56 * 1024 * 1024


def _split3(a):
    hi = a.astype(BF16)
    r = a - hi.astype(F32)
    mid = r.astype(BF16)
    lo = (r - mid.astype(F32)).astype(BF16)
    return hi, mid, lo


def _dot(a, b):
    return jnp.dot(a, b, preferred_element_type=F32)


def _dot_nt(a, b):
    return lax.dot_general(a, b, (((1,), (1,)), ((), ())), preferred_element_type=F32)


def _dot_tn(a, b):
    return lax.dot_general(a, b, (((0,), (0,)), ((), ())), preferred_element_type=F32)


def _sigmoid(v):
    return 1.0 / (1.0 + jnp.exp(-v))


def _silu(v):
    return v * _sigmoid(v)


def _ab_kernel(x_ref, g_ref, wza_ref, wxbc_ref, wzb_ref, wqk_ref, wv_ref, wog_ref, wsm_ref,
               scw_ref, scb_ref, smb_ref, sma_ref, dskip_ref, sng_ref,
               mcw_ref, mcb_ref, mng_ref, wout_ref, tril_ref, e16_ref,
               out_ref,
               xbcp, qkp, sst, cst, nst, mst, ybuf):
    T = T_AB
    c = pl.program_id(1)

    @pl.when(c == 0)
    def _():
        xbcp[0:SUBLANES, :] = jnp.zeros((SUBLANES, SSD_XBC), F32)
        qkp[0:SUBLANES, :] = jnp.zeros((SUBLANES, 2 * ML_QK), F32)
        sst[...] = jnp.zeros_like(sst)
        cst[...] = jnp.zeros_like(cst)
        nst[...] = jnp.zeros_like(nst)
        mst[...] = jnp.zeros_like(mst)

    x = x_ref[0]
    ms = jnp.mean(x * x, axis=-1, keepdims=True)
    h = (x * lax.rsqrt(ms + EPS) * g_ref[...]).astype(BF16)

    def conv(pad_ref, w_ref, cw_ref, cb_ref):
        cur = _dot(h, w_ref[...])
        pad_ref[SUBLANES:SUBLANES + T, :] = cur
        acc = cb_ref[...] + cw_ref[3:4, :] * cur
        for k in range(3):
            off = SUBLANES - 3 + k
            acc = acc + cw_ref[k:k + 1, :] * pad_ref[off:off + T, :]
        pad_ref[0:SUBLANES, :] = pad_ref[T:T + SUBLANES, :]
        return _silu(acc)

    xbc = conv(xbcp, wxbc_ref, scw_ref, scb_ref)
    qk = conv(qkp, wqk_ref, mcw_ref, mcb_ref)

    lane = lax.broadcasted_iota(jnp.int32, (T, LANES), 1)
    sv = _dot(h, wsm_ref[...]) + smb_ref[...]
    e = jnp.log1p(jnp.exp(-jnp.abs(sv)))
    dt = jnp.where(lane < SM_IG, jnp.maximum(sv, 0.0) + e, 0.0)
    fgl = jnp.minimum(sv, 0.0) - e
    cs_in = jnp.where(lane < SM_IG, dt * sma_ref[...],
                      jnp.where(lane < SM_FG, 0.0, jnp.where(lane < SM_FG + ML_HEADS, fgl, 0.0)))
    tril = tril_ref[...]
    c_hi, c_mid, c_lo = _split3(cs_in)
    cs = _dot(tril, c_hi) + _dot(tril, c_mid) + _dot(tril, c_lo)
    scol = jnp.where(lane < SM_IG, cs, jnp.where(lane < SM_FG, sv, cs))
    srow = scol.T
    cs_end = cs[T - 1:T, :]
    eacs = jnp.where(lane < SM_IG, jnp.exp(cs), 0.0)
    toend = jnp.where(lane < SM_IG, jnp.exp(cs_end - cs), 0.0)

    e16 = e16_ref[...]

    def expand(a):
        return _dot(jnp.concatenate(_split3(a), axis=1), e16)

    dt_x = expand(dt)
    eacs_x = expand(eacs)
    toend_x = expand(toend)

    rowi = lax.broadcasted_iota(jnp.int32, (T, T), 0)
    coli = lax.broadcasted_iota(jnp.int32, (T, T), 1)
    causal = rowi >= coli
    lane64 = lane < SSD_HEAD_DIM

    xs = xbc[:, :SSD_INNER]
    xdt = xs * dt_x
    xdt_end = (xdt * toend_x).astype(BF16)
    eend_x = eacs_x[T - 1:T, :]
    za = _dot(h, wza_ref[...])
    gate_a = _silu(za)
    for g in range(SSD_GROUPS):
        bm = xbc[:, SSD_INNER + g * SSD_STATE:SSD_INNER + (g + 1) * SSD_STATE].astype(BF16)
        cm = xbc[:, SSD_INNER + SSD_GN + g * SSD_STATE:SSD_INNER + SSD_GN + (g + 1) * SSD_STATE].astype(BF16)
        cb = _dot_nt(cm, bm)
        gs = slice(g * 512, (g + 1) * 512)
        s_old = sst[g]
        yoff = _dot(cm, s_old.astype(BF16)) * eacs_x[:, gs]
        for p in range(SSD_HPG // 2):
            hd0 = g * SSD_HPG + 2 * p
            ms_ = []
            for hd in (hd0, hd0 + 1):
                seg = scol[:, hd:hd + 1] - srow[hd:hd + 1, :]
                dec = jnp.exp(jnp.where(causal, seg, NEG))
                ms_.append((cb * dec).astype(BF16))
            mcat = jnp.concatenate(ms_, axis=1)
            ls = slice(hd0 * SSD_HEAD_DIM, (hd0 + 2) * SSD_HEAD_DIM)
            slab = xdt[:, ls]
            bd = jnp.concatenate([jnp.where(lane64, slab, 0.0).astype(BF16),
                                  jnp.where(lane64, 0.0, slab).astype(BF16)], axis=0)
            y = _dot(mcat, bd) + yoff[:, 2 * p * SSD_HEAD_DIM:(2 * p + 2) * SSD_HEAD_DIM]
            y = y + xs[:, ls] * dskip_ref[:, ls]
            ybuf[:, ls] = y * gate_a[:, ls]
        sst[g] = s_old * eend_x[:, gs] + _dot_tn(bm, xdt_end[:, gs])
    for g in range(SSD_GROUPS):
        gs = slice(g * 512, (g + 1) * 512)
        yg = ybuf[:, gs]
        inv = lax.rsqrt(jnp.mean(yg * yg, axis=-1, keepdims=True) + EPS)
        ybuf[:, gs] = yg * inv * sng_ref[:, gs]

    vv = _dot(h, wv_ref[...])
    og = _dot(h, wog_ref[...])
    zb = _dot(h, wzb_ref[...])
    for hd in range(ML_HEADS):
        q_h = qk[:, hd * ML_QK_DIM:(hd + 1) * ML_QK_DIM]
        k_h = qk[:, ML_QK + hd * ML_QK_DIM:ML_QK + (hd + 1) * ML_QK_DIM] * (ML_QK_DIM ** -0.5)
        vs = slice(hd * ML_V_DIM, (hd + 1) * ML_V_DIM)
        v_h = vv[:, vs].astype(BF16)
        q_b = q_h.astype(BF16)
        b_c = scol[:, SM_FG + hd:SM_FG + hd + 1]
        b_r = srow[SM_FG + hd:SM_FG + hd + 1, :]
        ig_c = scol[:, SM_IG + hd:SM_IG + hd + 1]
        ig_r = srow[SM_IG + hd:SM_IG + hd + 1, :]
        mp = mst[hd:hd + 1, 0:1]
        c_old = cst[hd]
        n_old = nst[hd:hd + 1, :]
        dlog = jnp.where(causal, b_c - b_r + ig_r, NEG)
        inter = b_c + mp
        m_t = jnp.maximum(inter, jnp.max(dlog, axis=1, keepdims=True))
        w_intra = jnp.exp(dlog - m_t)
        w_inter = jnp.exp(inter - m_t)
        s = _dot_nt(q_b, k_h.astype(BF16)) * w_intra
        num = _dot(s.astype(BF16), v_h) + w_inter * _dot(q_b, c_old.astype(BF16))
        den = jnp.sum(s, axis=1, keepdims=True) + w_inter * jnp.sum(q_h * n_old, axis=1, keepdims=True)
        hh = num / jnp.maximum(jnp.abs(den), jnp.exp(-m_t))
        b_end = b_c[T - 1:T, :]
        glog = b_end - b_c + ig_c
        m_new = jnp.maximum(b_end + mp, jnp.max(glog, axis=0, keepdims=True))
        kg = k_h * jnp.exp(glog - m_new)
        dec = jnp.exp(b_end + mp - m_new)
        cst[hd] = dec * c_old + _dot_tn(kg.astype(BF16), v_h)
        nst[hd:hd + 1, :] = dec * n_old + jnp.sum(kg, axis=0, keepdims=True)
        mst[hd:hd + 1, :] = jnp.broadcast_to(m_new, (1, LANES))
        ho = hh * _sigmoid(og[:, vs])
        inv = lax.rsqrt(jnp.mean(ho * ho, axis=-1, keepdims=True) + EPS)
        ybuf[:, ML_INNER + hd * ML_V_DIM:ML_INNER + (hd + 1) * ML_V_DIM] = (
            ho * inv * mng_ref[:, vs] * _silu(zb[:, vs]))

    y = _dot(ybuf[...].astype(BF16), wout_ref[...])
    out_ref[0] = x + y


def _const_spec(shape):
    nd = len(shape)
    return pl.BlockSpec(shape, lambda b, c: (0,) * nd, pipeline_mode=pl.Buffered(1))


def _layer_ab(x, norm_g, w_in, ssd_cw, ssd_cb, dt_bias, a_log, d_skip, ssd_ng,
              ml_cw, ml_cb, i_bias, f_bias, ml_ng, w_out):
    B, L, D = x.shape
    T = T_AB
    o = AB_OFFS
    seg = lambda i: w_in[:, o[i]:o[i + 1]]
    wza, wxbc, wdt, wzb, wq, wk, wv, wig, wfg, wog = [seg(i) for i in range(10)]
    wsm = jnp.concatenate([wdt, wig, wfg, jnp.zeros((D, LANES - SM_FG - ML_HEADS), F32)], axis=1)
    wqk = jnp.concatenate([wq, wk], axis=1)
    pad = jnp.zeros((LANES - SM_FG - ML_HEADS,), F32)
    smb = jnp.concatenate([dt_bias, i_bias, f_bias, pad])[None, :]
    sma = jnp.concatenate([-jnp.exp(a_log), jnp.zeros((LANES - SSD_HEADS,), F32)])[None, :]
    dskip = jnp.repeat(d_skip, SSD_HEAD_DIM)[None, :]
    tril = jnp.asarray(np.tril(np.ones((T, T), np.float32)), BF16)
    e16_np = np.zeros((LANES, SSD_INNER), np.float32)
    for hd in range(SSD_HEADS):
        e16_np[hd, hd * SSD_HEAD_DIM:(hd + 1) * SSD_HEAD_DIM] = 1.0
    e16 = jnp.asarray(np.concatenate([e16_np] * 3, axis=0), BF16)
    bf = lambda a: a.astype(BF16)
    row = lambda a: a[None, :]
    args = (x, row(norm_g), bf(wza), bf(wxbc), bf(wzb), bf(wqk), bf(wv), bf(wog), bf(wsm),
            ssd_cw, row(ssd_cb), smb, sma, dskip, row(ssd_ng),
            ml_cw, row(ml_cb), row(ml_ng), bf(w_out), tril, e16)
    in_specs = [pl.BlockSpec((1, T, D), lambda b, c: (b, c, 0))]
    in_specs += [_const_spec(a.shape) for a in args[1:]]
    return pl.pallas_call(
        _ab_kernel,
        grid=(B, L // T),
        in_specs=in_specs,
        out_specs=pl.BlockSpec((1, T, D), lambda b, c: (b, c, 0)),
        out_shape=jax.ShapeDtypeStruct((B, L, D), F32),
        scratch_shapes=[
            pltpu.VMEM((T + SUBLANES, SSD_XBC), F32),
            pltpu.VMEM((T + SUBLANES, 2 * ML_QK), F32),
            pltpu.VMEM((SSD_GROUPS, SSD_STATE, SSD_HPG * SSD_HEAD_DIM), F32),
            pltpu.VMEM((ML_HEADS, ML_QK_DIM, ML_V_DIM), F32),
            pltpu.VMEM((SUBLANES, ML_QK_DIM), F32),
            pltpu.VMEM((SUBLANES, LANES), F32),
            pltpu.VMEM((T, SSD_INNER + ML_INNER), F32),
        ],
        compiler_params=pltpu.CompilerParams(
            dimension_semantics=("arbitrary", "arbitrary"), vmem_limit_bytes=VMEM_LIMIT),
        name="layer_ab",
    )(*args)


T_Q = 256
T_S = 256
UNROLL = 4
N_AHEAD = UNROLL + 1
N_BUF = 16
V_ROWS = DA_V_DIM + 16
LAYER_C = 1
LAMBDA_INIT = 0.8 - 0.6 * math.exp(-0.3 * LAYER_C)
ALIBI_SLOPES = tuple(2.0 ** (-8.0 * (h + 1) / DA_HEADS) for h in range(DA_HEADS))
LOG2E = 1.4426950408889634


def _cproj_kernel(x_ref, g_ref, wq_ref, wk_ref, wv_ref, wz_ref, q_ref, k_ref, vt_ref, z_ref):
    x = x_ref[0]
    ms = jnp.mean(x * x, axis=-1, keepdims=True)
    h = (x * lax.rsqrt(ms + EPS) * g_ref[...]).astype(BF16)
    q_ref[0] = (_dot(h, wq_ref[...]) * (DA_QK_DIM ** -0.5 * LOG2E)).astype(BF16)
    z_ref[0] = _dot(h, wz_ref[...]).astype(BF16)
    kk = _dot(h, wk_ref[...])
    vv = _dot(h, wv_ref[...])
    tm = x.shape[0]
    for hd in range(DA_HEADS):
        ls = slice(hd * DA_V_DIM, (hd + 1) * DA_V_DIM)
        k_ref[0, hd] = kk[:, ls].astype(BF16)
        vt_ref[0, hd, 0, 0:DA_V_DIM, :] = vv[:, ls].T.astype(BF16)
        vt_ref[0, hd, 0, DA_V_DIM:V_ROWS, :] = jnp.ones((V_ROWS - DA_V_DIM, tm), BF16)


def _layer_c_proj(x, norm_g, w_in):
    B, L, D = x.shape
    tm = T_S
    wq, wk, wv, wz = [w_in[:, i * DA_QK:(i + 1) * DA_QK].astype(BF16) for i in range(4)]
    cspec = lambda shape: pl.BlockSpec(shape, lambda b, i: (0,) * len(shape), pipeline_mode=pl.Buffered(1))
    return pl.pallas_call(
        _cproj_kernel,
        grid=(B, L // tm),
        in_specs=[pl.BlockSpec((1, tm, D), lambda b, i: (b, i, 0)), cspec((1, D)),
                  cspec((D, DA_QK)), cspec((D, DA_QK)), cspec((D, DA_INNER)), cspec((D, DA_INNER))],
        out_specs=[pl.BlockSpec((1, tm, DA_QK), lambda b, i: (b, i, 0)),
                   pl.BlockSpec((1, DA_HEADS, tm, 2 * DA_QK_DIM), lambda b, i: (b, 0, i, 0)),
                   pl.BlockSpec((1, DA_HEADS, 1, V_ROWS, tm), lambda b, i: (b, 0, i, 0, 0)),
                   pl.BlockSpec((1, tm, DA_INNER), lambda b, i: (b, i, 0))],
        out_shape=[jax.ShapeDtypeStruct((B, L, DA_QK), BF16),
                   jax.ShapeDtypeStruct((B, DA_HEADS, L, 2 * DA_QK_DIM), BF16),
                   jax.ShapeDtypeStruct((B, DA_HEADS, L // tm, V_ROWS, tm), BF16),
                   jax.ShapeDtypeStruct((B, L, DA_INNER), BF16)],
        compiler_params=pltpu.CompilerParams(
            dimension_semantics=("arbitrary", "arbitrary"), vmem_limit_bytes=VMEM_LIMIT),
        name="layer_c_proj",
    )(x, norm_g[None, :], wq, wk, wv, wz)


def _attn_kernel(q_ref, k_hbm, vt_hbm, z_ref, x_ref, kaug_ref, qaug_ref, dbias_ref, cstep_ref, lamv_ref,
                 subg_ref, fng_ref, wout_ref, out_ref, kbuf, vbuf, sem, qt_scr, m_scr, acc_scr, st_scr, cmax_scr,
                 pt_scr, alpha_scr, ybuf):
    b = pl.program_id(0)
    i = pl.program_id(1)
    qk_rows = 2 * DA_QK_DIM

    def kv_copies(u, bb=b):
        slot = u % N_BUF
        rows = pl.ds(pl.multiple_of(u * T_S, T_S), T_S)
        return (pltpu.make_async_copy(k_hbm.at[bb, :, rows, :], kbuf.at[slot], sem.at[0, slot]),
                pltpu.make_async_copy(vt_hbm.at[bb, :, u], vbuf.at[slot], sem.at[1, slot]))

    def fetch(u, bb=b):
        for cp in kv_copies(u, bb):
            cp.start()

    def wait(u):
        for cp in kv_copies(u):
            cp.wait()

    def fetch_head(bb, ii):
        for u in range(N_AHEAD):
            @pl.when(u <= ii)
            def _():
                fetch(u, bb)

    @pl.when((b == 0) & (i == 0))
    def _():
        fetch_head(b, i)

    def init():
        m_scr[...] = jnp.full(m_scr.shape, NEG, F32)
        acc_scr[...] = jnp.zeros_like(acc_scr)
        pt_scr[...] = jnp.zeros_like(pt_scr)
        alpha_scr[...] = jnp.ones_like(alpha_scr)
        rowi = lax.broadcasted_iota(jnp.int32, (qk_rows, T_Q), 0)
        for hd in range(DA_HEADS):
            qht = q_ref[0, :, hd * qk_rows:(hd + 1) * qk_rows].astype(F32).T
            qt_scr[2 * hd, 0:qk_rows, :] = jnp.where(rowi < DA_QK_DIM, qht, 0.0).astype(BF16)
            qt_scr[2 * hd + 1, 0:qk_rows, :] = jnp.where(rowi < DA_QK_DIM, 0.0, qht).astype(BF16)
            qt_scr[2 * hd, qk_rows:, :] = qaug_ref[hd]
            qt_scr[2 * hd + 1, qk_rows:, :] = qaug_ref[hd]

    init()

    n_hh = 2 * DA_HEADS
    kaug = kaug_ref[...]

    def stage_a_full(hh, u):
        lhs = jnp.concatenate([kbuf[u % N_BUF, hh // 2], kaug], axis=1)
        st = _dot(lhs, qt_scr[hh])
        st_scr[hh] = st
        cmax_scr[hh] = jnp.max(st, axis=0, keepdims=True)

    def stage_a_diag(hh, u):
        st = _dot(kbuf[u % N_BUF, hh // 2], qt_scr[hh, 0:qk_rows, :]) + dbias_ref[hh // 2]
        st_scr[hh] = st
        cmax_scr[hh] = jnp.max(st, axis=0, keepdims=True)

    def stage_b(hh):
        m_old = m_scr[hh] - cstep_ref[hh // 2]
        m_new = jnp.maximum(m_old, cmax_scr[hh])
        alpha_scr[hh] = jnp.exp2(m_old - m_new)
        pt_scr[hh] = jnp.exp2(st_scr[hh] - m_new).astype(BF16)
        m_scr[hh] = m_new

    def stage_c(hh, u):
        acc_scr[hh] = acc_scr[hh] * alpha_scr[hh] + _dot(vbuf[u % N_BUF, hh // 2], pt_scr[hh])

    def body(u, n_sub, stage_a_last):
        for t in range(n_sub):
            wait(u + 1 + t)
        for t in range(n_sub):
            @pl.when(u + N_AHEAD + t <= i)
            def _():
                fetch(u + N_AHEAD + t)
        for t in range(n_sub):
            v = u + t
            vp = jnp.maximum(v - 1, 0)
            stage_a = stage_a_last if t == n_sub - 1 else stage_a_full
            for hh in range(n_hh):
                stage_c(hh, vp)
                stage_b(hh)
                stage_a(hh, v + 1)

    wait(0)

    @pl.when(i == 0)
    def _():
        for hh in range(n_hh):
            stage_a_diag(hh, 0)

    @pl.when(i > 0)
    def _():
        for hh in range(n_hh):
            stage_a_full(hh, 0)

        def loop_body(p, carry):
            body(UNROLL * p, UNROLL, stage_a_full)
            return carry

        n_loop = (i - 1) // UNROLL
        lax.fori_loop(0, n_loop, loop_body, 0)
        rem = (i - 1) % UNROLL
        done = n_loop * UNROLL
        piece = UNROLL // 2
        while piece >= 1:
            @pl.when(rem & piece != 0)
            def _(done=done, piece=piece):
                body(done, piece, stage_a_full)
            done = done + jnp.where(rem & piece != 0, piece, 0)
            piece //= 2

        body(i - 1, 1, stage_a_diag)

    u_prev = jnp.maximum(i - 1, 0)
    for hh in range(n_hh):
        stage_c(hh, u_prev)
        stage_b(hh)
    for hh in range(n_hh):
        stage_c(hh, i)

    n_b, n_i = pl.num_programs(0), pl.num_programs(1)
    wrap = i + 1 == n_i
    next_b = jnp.where(wrap, b + 1, b)
    next_i = jnp.where(wrap, 0, i + 1)

    @pl.when(next_b < n_b)
    def _():
        fetch_head(next_b, next_i)

    def epilogue():
        lv = lamv_ref[...]
        lam = (jnp.exp(jnp.sum(lv[0:1] * lv[1:2], axis=1, keepdims=True))
               - jnp.exp(jnp.sum(lv[2:3] * lv[3:4], axis=1, keepdims=True)) + LAMBDA_INIT)
        for hd in range(DA_HEADS):
            a1 = acc_scr[2 * hd]
            a2 = acc_scr[2 * hd + 1]
            o1 = a1[0:DA_V_DIM] / a1[DA_V_DIM:DA_V_DIM + 1]
            o2 = a2[0:DA_V_DIM] / a2[DA_V_DIM:DA_V_DIM + 1]
            ot = o1 - lam * o2
            inv = lax.rsqrt(jnp.mean(ot * ot, axis=0, keepdims=True) + EPS)
            ot = ot * inv * subg_ref[...] * (1.0 - LAMBDA_INIT)
            ls = slice(hd * DA_V_DIM, (hd + 1) * DA_V_DIM)
            ybuf[:, ls] = ot.T * _silu(z_ref[0, :, ls].astype(F32))
        y = _dot(ybuf[...].astype(BF16), wout_ref[...])
        xo = x_ref[0] + y
        inv = lax.rsqrt(jnp.mean(xo * xo, axis=-1, keepdims=True) + EPS)
        out_ref[0] = xo * inv * fng_ref[...]

    epilogue()


def _np_split3(c):
    rb = lambda a: a.astype(ml_dtypes.bfloat16).astype(np.float32)
    hi = rb(c)
    mid = rb(c - hi)
    lo = rb(c - hi - mid)
    return hi, mid, lo


def _attn_tables():
    c32 = (np.asarray(ALIBI_SLOPES, np.float64) * LOG2E).astype(np.float32)
    jj = np.arange(T_S, dtype=np.float32)[:, None]
    ii = np.arange(T_Q, dtype=np.float32)[None, :]
    allowed = (jj // CHUNK) <= (ii // CHUNK)
    kaug = np.zeros((T_S, LANES), np.float32)
    kaug[:, 0:3] = jj
    qaug = np.zeros((DA_HEADS, LANES, T_Q), np.float32)
    for r, part in enumerate(_np_split3(c32)):
        qaug[:, r, :] = part[:, None]
    dbias = np.where(allowed[None], c32[:, None, None] * (ii - np.abs(ii - jj))[None], np.float32(NEG))
    cstep = np.broadcast_to((c32 * T_S)[:, None, None], (DA_HEADS, 1, T_Q))
    return (jnp.asarray(kaug, BF16), jnp.asarray(qaug, BF16), jnp.asarray(dbias, F32), jnp.asarray(cstep, F32))


def _layer_c_attn(x, q, k, vt, z, lam_q1, lam_k1, lam_q2, lam_k2, subln_g, w_out, final_g):
    B, L, D = x.shape
    lamv = jnp.zeros((SUBLANES, LANES), F32)
    for r, v in enumerate((lam_q1, lam_k1, lam_q2, lam_k2)):
        lamv = lamv.at[r, :DA_QK_DIM].set(v)
    kaug, qaug, dbias, cstep = _attn_tables()
    cspec = lambda shape: pl.BlockSpec(shape, lambda b, i: (0,) * len(shape), pipeline_mode=pl.Buffered(1))
    row_spec = pl.BlockSpec((1, T_Q, D), lambda b, i: (b, i, 0))
    hbm_spec = pl.BlockSpec(memory_space=pl.ANY)
    in_specs = [row_spec, hbm_spec, hbm_spec, row_spec, row_spec,
                cspec((T_S, LANES)), cspec((DA_HEADS, LANES, T_Q)), cspec((DA_HEADS, T_S, T_Q)),
                cspec((DA_HEADS, 1, T_Q)), cspec((SUBLANES, LANES)), cspec((DA_V_DIM, 1)),
                cspec((1, D)), cspec((DA_INNER, D))]
    return pl.pallas_call(
        _attn_kernel,
        grid=(B, L // T_Q),
        in_specs=in_specs,
        out_specs=row_spec,
        scratch_shapes=_attn_scratch(),
        out_shape=jax.ShapeDtypeStruct((B, L, D), F32),
        compiler_params=pltpu.CompilerParams(
            dimension_semantics=("arbitrary", "arbitrary"), vmem_limit_bytes=VMEM_LIMIT),
        name="layer_c_attn",
    )(q, k, vt, z, x, kaug, qaug, dbias, cstep, lamv, subln_g[:, None], final_g[None, :], w_out.astype(BF16))


def _attn_scratch():
    n_hh = 2 * DA_HEADS
    return [
        pltpu.VMEM((N_BUF, DA_HEADS, T_S, 2 * DA_QK_DIM), BF16),
        pltpu.VMEM((N_BUF, DA_HEADS, V_ROWS, T_S), BF16),
        pltpu.SemaphoreType.DMA((2, N_BUF)),
        pltpu.VMEM((n_hh, 2 * DA_QK_DIM + LANES, T_Q), BF16),
        pltpu.VMEM((n_hh, 1, T_Q), F32),
        pltpu.VMEM((n_hh, V_ROWS, T_Q), F32),
        pltpu.VMEM((n_hh, T_S, T_Q), F32),
        pltpu.VMEM((n_hh, 1, T_Q), F32),
        pltpu.VMEM((n_hh, T_S, T_Q), BF16),
        pltpu.VMEM((n_hh, 1, T_Q), F32),
        pltpu.VMEM((T_Q, DA_INNER), F32),
    ]


def kernel(x, ab_norm_g, ab_w_in, ab_ssd_conv_w, ab_ssd_conv_b, ab_dt_bias, ab_a_log, ab_d_skip, ab_ssd_norm_g,
           ab_ml_conv_w, ab_ml_conv_b, ab_i_bias, ab_f_bias, ab_ml_norm_g, ab_w_out,
           c_norm_g, c_w_in, c_lam_q1, c_lam_k1, c_lam_q2, c_lam_k2, c_subln_g, c_w_out, final_norm_g):
    assert x.ndim == 3 and x.shape[-1] == DA_INNER and x.dtype == F32, (x.shape, x.dtype)
    assert x.shape[1] % T_AB == 0 and x.shape[1] % T_Q == 0 and T_Q == T_S, x.shape
    assert ab_w_in.shape == (1, x.shape[-1], AB_OFFS[-1]) and c_w_in.shape == (1, x.shape[-1], 4 * DA_QK)
    x1 = _layer_ab(x, ab_norm_g[0], ab_w_in[0], ab_ssd_conv_w[0], ab_ssd_conv_b[0], ab_dt_bias[0], ab_a_log[0],
                   ab_d_skip[0], ab_ssd_norm_g[0], ab_ml_conv_w[0], ab_ml_conv_b[0], ab_i_bias[0], ab_f_bias[0],
                   ab_ml_norm_g[0], ab_w_out[0])
    q, k, vt, z = _layer_c_proj(x1, c_norm_g[0], c_w_in[0])
    return _layer_c_attn(x1, q, k, vt, z, c_lam_q1[0], c_lam_k1[0], c_lam_q2[0], c_lam_k2[0],
                         c_subln_g[0], c_w_out[0], final_norm_g)
```
